```python
import math
import jax
import jax.numpy as jnp
from jax import lax
import numpy as np

D_MODEL = 4096
BATCH = 4
SEQ = 4096
DEPTH = 4

CTX_LEN = 256
GRID_W = 64
CHUNK = 128
BRANCH_WIDTH = D_MODEL // 4
A_WIDTH = BRANCH_WIDTH
A_GROUP = 128
A_HEADS = A_WIDTH // A_GROUP
B_WIDTH = BRANCH_WIDTH
B_GROUP = 16
B_GROUPS = B_WIDTH // B_GROUP
B_STATE = 64
N_DIR = 2
C_WIDTH = BRANCH_WIDTH
POOL_WINDOWS = (2, 4, 8, 16)
C_GROUP = C_WIDTH // len(POOL_WINDOWS)
N_BRANCH = 3
PROJ_WIDTH = 2 * A_WIDTH + B_WIDTH + C_WIDTH
ADA_RANK = D_MODEL // 4
N_MOD = 6
N_EXPERTS = 16
EXPERT_FF = D_MODEL // 8
CAPACITY_FACTOR = 2
LN_EPS = 1e-6
DEEPNORM_ALPHA = (2.0 * DEPTH) ** 0.25
DEEPNORM_BETA = (8.0 * DEPTH) ** -0.25

kernel_name = 'hybrid_gmlp_s5_pool_ecmoe_diffusion_trunk'


def layer_norm(x, gain, bias):
    xf = x.astype(jnp.float32)
    mu = jnp.mean(xf, -1, keepdims=True)
    var = jnp.mean(jnp.square(xf - mu), -1, keepdims=True)
    y = (xf - mu) * lax.rsqrt(var + LN_EPS)
    return (y * gain.astype(jnp.float32) + bias.astype(jnp.float32)).astype(x.dtype)


def plain_layer_norm(x):
    xf = x.astype(jnp.float32)
    mu = jnp.mean(xf, -1, keepdims=True)
    var = jnp.mean(jnp.square(xf - mu), -1, keepdims=True)
    return ((xf - mu) * lax.rsqrt(var + LN_EPS)).astype(x.dtype)


def ada_modulation(cond, w_down, w_up, b_up):
    m = (jax.nn.silu(cond) @ w_down) @ w_up + b_up
    return jnp.split(m[..., None, :], N_MOD, axis=-1)


def modulate(x, shift, scale):
    return x * (1.0 + scale) + shift


def post_norm_residual(x, sub, gate, gain, bias):
    return layer_norm(DEEPNORM_ALPHA * x + gate * sub, gain, bias)


def chunk_gmlp(u, v, w_s, b_s):
    bsz, length, _ = v.shape
    vn = plain_layer_norm(v).reshape(bsz, length // CHUNK, CHUNK, A_HEADS, A_GROUP)
    mixed = jnp.einsum('hts,bnshc->bnthc', w_s, vn) + b_s.T[:, :, None]
    return u * mixed.reshape(bsz, length, A_WIDTH)


def s5_discretise(a_re, a_im, log_step, b_re, b_im):
    a_re = a_re.astype(jnp.float32)
    a_im = a_im.astype(jnp.float32)
    step = jnp.exp(log_step.astype(jnp.float32))[:, None]
    mag = jnp.exp(a_re * step)
    lam_re = mag * jnp.cos(a_im * step)
    lam_im = mag * jnp.sin(a_im * step)
    den = a_re * a_re + a_im * a_im
    f_re = ((lam_re - 1.0) * a_re + lam_im * a_im) / den
    f_im = (lam_im * a_re - (lam_re - 1.0) * a_im) / den
    b_re = b_re.astype(jnp.float32)
    b_im = b_im.astype(jnp.float32)
    bb_re = f_re[..., None] * b_re - f_im[..., None] * b_im
    bb_im = f_re[..., None] * b_im + f_im[..., None] * b_re
    return lam_re, lam_im, bb_re, bb_im


def _complex_scan_combine(e1, e2):
    a1r, a1i, b1r, b1i = e1
    a2r, a2i, b2r, b2i = e2
    return (a2r * a1r - a2i * a1i, a2r * a1i + a2i * a1r,
            a2r * b1r - a2i * b1i + b2r, a2r * b1i + a2i * b1r + b2i)


def diagonal_scan(lam_re, lam_im, bu_re, bu_im, h0_re, h0_im):
    bu_re = bu_re.at[:, 0].add(lam_re * h0_re - lam_im * h0_im)
    bu_im = bu_im.at[:, 0].add(lam_re * h0_im + lam_im * h0_re)
    a_re = jnp.broadcast_to(lam_re, bu_re.shape)
    a_im = jnp.broadcast_to(lam_im, bu_im.shape)
    _, _, h_re, h_im = lax.associative_scan(_complex_scan_combine, (a_re, a_im, bu_re, bu_im), axis=1)
    return h_re, h_im


def s5_states(u, h0, p):
    bsz, length, _ = u.shape
    ug = u.astype(jnp.float32).reshape(bsz, length, B_GROUPS, B_GROUP)
    states = []
    for d in range(N_DIR):
        lam_re, lam_im, bb_re, bb_im = s5_discretise(p['s5_a_re'][d], p['s5_a_im'][d], p['s5_log_step'][d],
                                                     p['s5_b_re'], p['s5_b_im'])
        ud = ug if d == 0 else ug[:, ::-1]
        bu_re = jnp.einsum('blgh,gph->blgp', ud, bb_re)
        bu_im = jnp.einsum('blgh,gph->blgp', ud, bb_im)
        h_re, h_im = diagonal_scan(lam_re, lam_im, bu_re, bu_im, h0[d][0], h0[d][1])
        if d == 1:
            h_re, h_im = h_re[:, ::-1], h_im[:, ::-1]
        states.append((h_re, h_im))
    return states


def s5_final_states(states):
    (f_re, f_im), (b_re, b_im) = states
    return ((f_re[:, -1], f_im[:, -1]), (b_re[:, 0], b_im[:, 0]))


def s5_readout(u, states, p):
    bsz, length, _ = u.shape
    y = u.astype(jnp.float32) * p['s5_d'].astype(jnp.float32)
    for d, (h_re, h_im) in enumerate(states):
        c_re = p['s5_c_re'][d].astype(jnp.float32)
        c_im = p['s5_c_im'][d].astype(jnp.float32)
        yd = jnp.einsum('blgp,ghp->blgh', h_re, c_re) - jnp.einsum('blgp,ghp->blgh', h_im, c_im)
        y = y + yd.reshape(bsz, length, B_WIDTH)
    g = jax.nn.gelu(y.astype(u.dtype))
    return g * jax.nn.sigmoid(g @ p['s5_w_glu'] + p['s5_b_glu'])


def centred_window_mean(x, k):
    length = x.shape[-2]
    t = np.arange(length)
    lo = np.clip(t - k // 2, 0, length - 1)
    hi = np.clip(t + k // 2 - 1, 0, length - 1)
    cnt = (hi - lo + 1).astype(np.float32)[:, None]
    cs = jnp.cumsum(x.astype(jnp.float32), axis=-2)
    cs = jnp.concatenate([jnp.zeros_like(cs[..., :1, :]), cs], axis=-2)
    s = jnp.take(cs, hi + 1, axis=-2) - jnp.take(cs, lo, axis=-2)
    return (s / cnt).astype(x.dtype)


def pool_mixer(x, w_pool, pool_scale, rows):
    bsz, length, _ = x.shape
    xs = x if rows is None else x.reshape(bsz, rows, GRID_W, C_WIDTH)
    groups = jnp.split(xs, len(POOL_WINDOWS), axis=-1)
    y = jnp.stack([centred_window_mean(g, k) - g for g, k in zip(groups, POOL_WINDOWS)], axis=-2)
    y = jnp.einsum('...gc,gcd->...gd', y, w_pool)
    return y.reshape(bsz, length, C_WIDTH) * pool_scale


def token_mixer(h, h0, rows, p):
    bsz, length, _ = h.shape
    proj = h @ p['w_in']
    pu, pv, pb, pc = jnp.split(proj, [A_WIDTH, 2 * A_WIDTH, 2 * A_WIDTH + B_WIDTH], axis=-1)
    y_a = chunk_gmlp(jax.nn.gelu(pu), jax.nn.gelu(pv), p['gmlp_w_s'], p['gmlp_b_s'])
    states = s5_states(pb, h0, p)
    y_b = s5_readout(pb, states, p)
    y_c = pool_mixer(pc, p['pool_w'], p['pool_scale'], rows)
    gates = jax.nn.sigmoid(h @ p['w_gate'] + p['b_gate']).reshape(bsz, length, N_BRANCH, D_MODEL)
    merged = sum(gates[:, :, k] * (y_k @ p['w_branch'][k]) for k, y_k in enumerate((y_a, y_b, y_c)))
    return merged @ p['w_out'], s5_final_states(states)


def context_scan_finals(hc, h0, p):
    pb = hc @ p['w_in'][:, 2 * A_WIDTH:2 * A_WIDTH + B_WIDTH]
    return s5_final_states(s5_states(pb, h0, p))


def expert_choice_ffn(h, w_router, w1, w3, w2):
    bsz, length, _ = h.shape
    cap = CAPACITY_FACTOR * length // N_EXPERTS
    aff = jax.nn.softmax((h @ w_router).astype(jnp.float32), axis=-1)
    top_w, top_idx = lax.top_k(jnp.swapaxes(aff, 1, 2), cap)
    xs = jax.vmap(lambda hb, ib: hb[ib])(h, top_idx)
    hid = jax.nn.silu(jnp.einsum('becd,edf->becf', xs, w1)) * jnp.einsum('becd,edf->becf', xs, w3)
    out_e = jnp.einsum('becf,efd->becd', hid, w2) * top_w.astype(h.dtype)[..., None]
    return jax.vmap(lambda ib, ob: jnp.zeros((length, D_MODEL), ob.dtype)
                    .at[ib.reshape(-1)].add(ob.reshape(-1, D_MODEL)))(top_idx, out_e)


def setup_inputs(seed: int = 0) -> dict:
    key = jax.random.key(seed)
    k = jax.random.split(key, 34)
    f32 = jnp.float32

    def nrm(i, shape, scale):
        return jax.random.normal(k[i], shape, f32) * scale

    L = DEPTH
    return {
        'x': nrm(0, (BATCH, SEQ, D_MODEL), 1.0),
        'c': nrm(1, (BATCH, D_MODEL), 1.0),
        'ctx': nrm(2, (BATCH, CTX_LEN, D_MODEL), 1.0),
        'c_ctx': nrm(3, (D_MODEL,), 1.0),
        'ada_w_down': nrm(4, (L, D_MODEL, ADA_RANK), D_MODEL ** -0.5),
        'ada_w_up': nrm(5, (L, ADA_RANK, N_MOD * D_MODEL), 0.5 * ADA_RANK ** -0.5),
        'ada_b': nrm(6, (L, N_MOD * D_MODEL), 0.02),
        'w_in': nrm(7, (L, D_MODEL, PROJ_WIDTH), D_MODEL ** -0.5),
        'gmlp_w_s': nrm(8, (L, A_HEADS, CHUNK, CHUNK), CHUNK ** -0.5),
        'gmlp_b_s': 1.0 + nrm(9, (L, A_HEADS, CHUNK), 0.01),
        's5_a_re': -0.5 + nrm(10, (L, N_DIR, B_GROUPS, B_STATE), 0.01),
        's5_a_im': jnp.pi * jnp.arange(B_STATE, dtype=f32) + nrm(11, (L, N_DIR, B_GROUPS, B_STATE), 0.01),
        's5_log_step': jax.random.uniform(k[12], (L, N_DIR, B_GROUPS), f32, math.log(1e-3), math.log(1e-1)),
        's5_b_re': nrm(13, (L, B_GROUPS, B_STATE, B_GROUP), (2.0 * B_GROUP) ** -0.5),
        's5_b_im': nrm(14, (L, B_GROUPS, B_STATE, B_GROUP), (2.0 * B_GROUP) ** -0.5),
        's5_c_re': nrm(15, (L, N_DIR, B_GROUPS, B_GROUP, B_STATE), (2.0 * B_STATE) ** -0.5),
        's5_c_im': nrm(16, (L, N_DIR, B_GROUPS, B_GROUP, B_STATE), (2.0 * B_STATE) ** -0.5),
        's5_d': nrm(17, (L, B_WIDTH), 1.0),
        's5_w_glu': nrm(18, (L, B_WIDTH, B_WIDTH), B_WIDTH ** -0.5),
        's5_b_glu': nrm(19, (L, B_WIDTH), 0.02),
        'pool_w': nrm(20, (L, len(POOL_WINDOWS), C_GROUP, C_GROUP), C_GROUP ** -0.5),
        'pool_scale': 1.0 + nrm(21, (L, C_WIDTH), 0.02),
        'w_gate': nrm(22, (L, D_MODEL, N_BRANCH * D_MODEL), D_MODEL ** -0.5),
        'b_gate': nrm(23, (L, N_BRANCH * D_MODEL), 0.02),
        'w_branch': nrm(24, (L, N_BRANCH, BRANCH_WIDTH, D_MODEL), DEEPNORM_BETA * BRANCH_WIDTH ** -0.5),
        'w_out': nrm(25, (L, D_MODEL, D_MODEL), DEEPNORM_BETA * D_MODEL ** -0.5),
        'ln1_g': 1.0 + nrm(26, (L, D_MODEL), 0.02),
        'ln1_b': nrm(27, (L, D_MODEL), 0.02),
        'w_router': nrm(28, (L, D_MODEL, N_EXPERTS), D_MODEL ** -0.5),
        'w1': nrm(29, (L, N_EXPERTS, D_MODEL, EXPERT_FF), D_MODEL ** -0.5),
        'w3': nrm(30, (L, N_EXPERTS, D_MODEL, EXPERT_FF), DEEPNORM_BETA * D_MODEL ** -0.5),
        'w2': nrm(31, (L, N_EXPERTS, EXPERT_FF, D_MODEL), DEEPNORM_BETA * EXPERT_FF ** -0.5),
        'ln2_g': 1.0 + nrm(32, (L, D_MODEL), 0.02),
        'ln2_b': nrm(33, (L, D_MODEL), 0.02),
    }


def reference(x, c, ctx, c_ctx, ada_w_down, ada_w_up, ada_b, w_in, gmlp_w_s, gmlp_b_s,
              s5_a_re, s5_a_im, s5_log_step, s5_b_re, s5_b_im, s5_c_re, s5_c_im, s5_d,
              s5_w_glu, s5_b_glu, pool_w, pool_scale, w_gate, b_gate, w_branch, w_out,
              ln1_g, ln1_b, w_router, w1, w3, w2, ln2_g, ln2_b):
    rows = x.shape[1] // GRID_W
    h0_zero = jnp.zeros((ctx.shape[0], B_GROUPS, B_STATE), jnp.float32)
    ctx_h0 = ((h0_zero, h0_zero), (h0_zero, h0_zero))
    xl, xc = x, ctx
    for l in range(DEPTH):
        p = dict(w_in=w_in[l], gmlp_w_s=gmlp_w_s[l], gmlp_b_s=gmlp_b_s[l],
                 s5_a_re=s5_a_re[l], s5_a_im=s5_a_im[l], s5_log_step=s5_log_step[l],
                 s5_b_re=s5_b_re[l], s5_b_im=s5_b_im[l], s5_c_re=s5_c_re[l], s5_c_im=s5_c_im[l],
                 s5_d=s5_d[l], s5_w_glu=s5_w_glu[l], s5_b_glu=s5_b_glu[l],
                 pool_w=pool_w[l], pool_scale=pool_scale[l], w_gate=w_gate[l], b_gate=b_gate[l],
                 w_branch=w_branch[l], w_out=w_out[l])
        sh1, sc1, g1, sh2, sc2, g2 = ada_modulation(c, ada_w_down[l], ada_w_up[l], ada_b[l])
        csh1, csc1, cg1, csh2, csc2, cg2 = ada_modulation(c_ctx, ada_w_down[l], ada_w_up[l], ada_b[l])
        hc = modulate(xc, csh1, csc1)
        hl = modulate(xl, sh1, sc1)
        if l < DEPTH - 1:
            mix_c, ctx_final = token_mixer(hc, ctx_h0, None, p)
            xc = post_norm_residual(xc, mix_c, cg1, ln1_g[l], ln1_b[l])
            ffn_c = expert_choice_ffn(modulate(xc, csh2, csc2), w_router[l], w1[l], w3[l], w2[l])
            xc = post_norm_residual(xc, ffn_c, cg2, ln2_g[l], ln2_b[l])
        else:
            ctx_final = context_scan_finals(hc, ctx_h0, p)
        mix_l, _ = token_mixer(hl, ctx_final, rows, p)
        xl = post_norm_residual(xl, mix_l, g1, ln1_g[l], ln1_b[l])
        ffn_l = expert_choice_ffn(modulate(xl, sh2, sc2), w_router[l], w1[l], w3[l], w2[l])
        xl = post_norm_residual(xl, ffn_l, g2, ln2_g[l], ln2_b[l])
    return xl
```

```python
import functools
import math

import jax
import jax.numpy as jnp
from jax import lax
from jax.experimental import pallas as pl
from jax.experimental.pallas import tpu as pltpu

F32 = jnp.float32
BF16 = jnp.bfloat16

LN_EPS = 1e-6
GRID_W = 64
CHUNK = 128
BRANCH = 1024
HEADS = 8
S5_GROUP = 16
S5_STATE = 64
S5_GB = 8
POOL_WINDOWS = (2, 4, 8, 16)
POOL_GROUP = BRANCH // len(POOL_WINDOWS)
N_MOD = 6
ROUTER_LANES = 128
CAPACITY_FACTOR = 2

V7X_VMEM_LIMIT = 60 * 1024 * 1024
SUBLANES = 8


def _cparams(*sem):
    return pltpu.CompilerParams(dimension_semantics=sem, vmem_limit_bytes=V7X_VMEM_LIMIT)


def _sigmoid(x):
    return 1.0 / (1.0 + jnp.exp(-x))


def _gelu(x):
    c = math.sqrt(2.0 / math.pi)
    return x * (0.5 * (1.0 + jnp.tanh(c * (x + 0.044715 * (x * x * x)))))


def _dot(a, b):
    return jnp.dot(a, b, preferred_element_type=F32)


def _ada_down_kernel(c_ref, w_ref, o_ref):
    @pl.when(pl.program_id(1) == 0)
    def _():
        o_ref[...] = jnp.zeros_like(o_ref)

    c = c_ref[...]
    o_ref[0] += _dot((c * _sigmoid(c)).astype(BF16), w_ref[0].astype(BF16))


def _ada_up_kernel(r_ref, w_ref, b_ref, o_ref):
    o_ref[0] = _dot(r_ref[0].astype(BF16), w_ref[0].astype(BF16)) + b_ref[0]


def _ada(cond, w_down, w_up, b_up):
    depth, d, rank = w_down.shape
    nmod = w_up.shape[2]
    tk, tn = 1024, 2048
    r = pl.pallas_call(
        _ada_down_kernel,
        out_shape=jax.ShapeDtypeStruct((depth, SUBLANES, rank), F32),
        grid=(depth, d // tk),
        in_specs=[pl.BlockSpec((SUBLANES, tk), lambda l, k: (0, k)),
                  pl.BlockSpec((1, tk, rank), lambda l, k: (l, k, 0))],
        out_specs=pl.BlockSpec((1, SUBLANES, rank), lambda l, k: (l, 0, 0)),
        compiler_params=_cparams("parallel", "arbitrary"),
        name="ada_down",
    )(cond, w_down)
    return pl.pallas_call(
        _ada_up_kernel,
        out_shape=jax.ShapeDtypeStruct((depth, SUBLANES, nmod), F32),
        grid=(depth, nmod // tn),
        in_specs=[pl.BlockSpec((1, SUBLANES, rank), lambda l, n: (l, 0, 0)),
                  pl.BlockSpec((1, rank, tn), lambda l, n: (l, 0, n)),
                  pl.BlockSpec((1, 1, tn), lambda l, n: (l, 0, n))],
        out_specs=pl.BlockSpec((1, SUBLANES, tn), lambda l, n: (l, 0, n)),
        compiler_params=_cparams("parallel", "parallel"),
        name="ada_up",
    )(r, w_up, b_up.reshape(depth, 1, nmod))


def _modulate_kernel(x_ref, sh_ref, sc_ref, o_ref):
    o_ref[...] = (x_ref[...] * (1.0 + sc_ref[0]) + sh_ref[0]).astype(BF16)


def _modulate(x, sh, sc, seq_len):
    t, d = x.shape
    tm = min(512, seq_len)
    per = seq_len // tm
    vec = pl.BlockSpec((1, 1, d), lambda i: (i // per, 0, 0))
    return pl.pallas_call(
        _modulate_kernel,
        out_shape=jax.ShapeDtypeStruct((t, d), BF16),
        grid=(t // tm,),
        in_specs=[pl.BlockSpec((tm, d), lambda i: (i, 0)), vec, vec],
        out_specs=pl.BlockSpec((tm, d), lambda i: (i, 0)),
        compiler_params=_cparams("parallel"),
        name="modulate",
    )(x, sh, sc)


def _proj_kernel(a_ref, w_ref, o_ref):
    j = pl.program_id(0)
    acc = _dot(a_ref[...], w_ref[...])

    @pl.when(j == 0)
    def _():
        o_ref[...] = _gelu(acc).astype(BF16)

    @pl.when(j == 1)
    def _():
        g = _gelu(acc)
        dlt = g - jnp.mean(g, axis=-1, keepdims=True)
        var = jnp.mean(dlt * dlt, axis=-1, keepdims=True)
        o_ref[...] = (dlt * lax.rsqrt(var + LN_EPS)).astype(BF16)

    @pl.when(j >= 2)
    def _():
        o_ref[...] = acc.astype(BF16)


def _proj(h, w_in):
    t, d = h.shape
    n = w_in.shape[1]
    tm = min(512, t)
    return pl.pallas_call(
        _proj_kernel,
        out_shape=jax.ShapeDtypeStruct((t, n), BF16),
        grid=(n // BRANCH, t // tm),
        in_specs=[pl.BlockSpec((tm, d), lambda j, i: (i, 0)),
                  pl.BlockSpec((d, BRANCH), lambda j, i: (0, j))],
        out_specs=pl.BlockSpec((tm, BRANCH), lambda j, i: (i, j)),
        compiler_params=_cparams("parallel", "parallel"),
        name="proj",
    )(h, w_in)


def _s5_kernel(u_ref, bm_ref, cm_ref, lam_ref, h0_ref, y_ref, hfin_ref, bu_ref, st_ref, *, tblk):
    i = pl.program_id(0)
    rows = tblk * SUBLANES
    half = S5_GB * S5_STATE

    @pl.when(i == 0)
    def _():
        st_ref[...] = h0_ref[...]

    q = lax.broadcasted_iota(jnp.int32, (rows, 128), 0) % SUBLANES
    is_fwd = q < SUBLANES // 2
    m_fwd = is_fwd.astype(F32).astype(BF16)
    m_bwd = (1.0 - is_fwd.astype(F32)).astype(BF16)

    for gb in range(S5_GB):
        ub = u_ref[:, gb * 128:(gb + 1) * 128]
        lhs = jnp.concatenate([ub * m_fwd, ub * m_bwd], axis=1)
        bu_ref[...] = _dot(lhs, bm_ref[gb])
        lr = lam_ref[gb, 0]
        li = lam_ref[gb, 1]

        def step(t, carry):
            hr, hi = carry
            r0 = pl.multiple_of(t * SUBLANES, SUBLANES)
            nr = lr * hr - li * hi + bu_ref[pl.ds(r0, SUBLANES), 0:half]
            ni = lr * hi + li * hr + bu_ref[pl.ds(r0, SUBLANES), half:2 * half]
            bu_ref[pl.ds(r0, SUBLANES), 0:half] = nr
            bu_ref[pl.ds(r0, SUBLANES), half:2 * half] = ni
            return nr, ni

        hr, hi = lax.fori_loop(0, tblk, step, (st_ref[gb, 0], st_ref[gb, 1]), unroll=8)
        st_ref[gb, 0] = hr
        st_ref[gb, 1] = hi
        yy = _dot(bu_ref[...].astype(BF16), cm_ref[gb])
        y_ref[:, gb * 128:(gb + 1) * 128] = jnp.where(is_fwd, yy[:, :128], yy[:, 128:])

    @pl.when(i == pl.num_programs(0) - 1)
    def _():
        hfin_ref[...] = st_ref[...]


def _s5_operators(a_re, a_im, log_step, b_re, b_im, c_re, c_im):
    a_re = a_re.astype(F32)
    a_im = a_im.astype(F32)
    step = jnp.exp(log_step.astype(F32))[..., None]
    mag = jnp.exp(a_re * step)
    lam_re = mag * jnp.cos(a_im * step)
    lam_im = mag * jnp.sin(a_im * step)
    den = a_re * a_re + a_im * a_im
    f_re = ((lam_re - 1.0) * a_re + lam_im * a_im) / den
    f_im = (lam_im * a_re - (lam_re - 1.0) * a_im) / den
    b_re = b_re.astype(F32)[None]
    b_im = b_im.astype(F32)[None]
    bb_re = f_re[..., None] * b_re - f_im[..., None] * b_im
    bb_im = f_re[..., None] * b_im + f_im[..., None] * b_re
    ngl = 128 // S5_GROUP
    eye = jnp.eye(ngl, dtype=F32)
    bb = jnp.stack([bb_re, bb_im], axis=1)
    bb = bb.reshape(2, 2, S5_GB, ngl, S5_STATE, S5_GROUP)
    bmat = jnp.einsum('drgaph,ab->gdahrbp', bb, eye).reshape(S5_GB, 2 * 128, 2 * ngl * S5_STATE)
    cc = jnp.stack([c_re.astype(F32), -c_im.astype(F32)], axis=1)
    cc = cc.reshape(2, 2, S5_GB, ngl, S5_GROUP, S5_STATE)
    cmat = jnp.einsum('drgahp,ab->grapdbh', cc, eye).reshape(S5_GB, 2 * ngl * S5_STATE, 2 * 128)
    lam = jnp.stack([lam_re, lam_im], axis=1)
    lam = lam.reshape(2, 2, S5_GB, ngl * S5_STATE).transpose(2, 1, 0, 3)
    lam = jnp.repeat(lam, SUBLANES // 2, axis=2)
    return bmat.astype(BF16), cmat.astype(BF16), lam


def _s5(pb, h0, ops):
    bs, ls, width = pb.shape
    assert 2 * bs == SUBLANES
    bmat, cmat, lam = ops
    tblk = 64
    rows = tblk * SUBLANES
    u = jnp.concatenate([pb.transpose(1, 0, 2), pb[:, ::-1].transpose(1, 0, 2)], axis=1)
    u = u.reshape(ls * SUBLANES, width)
    st_shape = (S5_GB, 2, SUBLANES, S5_GB * S5_STATE)
    full = lambda shape: pl.BlockSpec(shape, lambda i: (0,) * len(shape))
    y, hfin = pl.pallas_call(
        functools.partial(_s5_kernel, tblk=tblk),
        out_shape=(jax.ShapeDtypeStruct((ls * SUBLANES, width), F32),
                   jax.ShapeDtypeStruct(st_shape, F32)),
        grid=(ls // tblk,),
        in_specs=[pl.BlockSpec((rows, width), lambda i: (i, 0)),
                  full(bmat.shape), full(cmat.shape), full(lam.shape), full(st_shape)],
        out_specs=(pl.BlockSpec((rows, width), lambda i: (i, 0)), full(st_shape)),
        scratch_shapes=[pltpu.VMEM((rows, 2 * S5_GB * S5_STATE), F32), pltpu.VMEM(st_shape, F32)],
        compiler_params=_cparams("arbitrary"),
        name="s5_scan",
    )(u, bmat, cmat, lam, h0)
    y = y.reshape(ls, SUBLANES, width)
    ys = y[:, :bs].transpose(1, 0, 2) + y[::-1, bs:].transpose(1, 0, 2)
    return ys, hfin


def _mixers_kernel(p_ref, ys_ref, ws_ref, bs_ref, sd_ref, wglu_ref, bglu_ref,
                   pm_ref, ic_ref, pw_ref, ps_ref, o_ref, *, tm):
    for ch in range(tm // CHUNK):
        r = slice(ch * CHUNK, (ch + 1) * CHUNK)
        for h in range(HEADS):
            vn = p_ref[r, BRANCH + h * 128:BRANCH + (h + 1) * 128]
            mixed = _dot(ws_ref[h], vn) + bs_ref[h]
            o_ref[r, h * 128:(h + 1) * 128] = (p_ref[r, h * 128:(h + 1) * 128].astype(F32) * mixed).astype(BF16)
    y = ys_ref[...] + p_ref[:, 2 * BRANCH:3 * BRANCH].astype(F32) * sd_ref[...]
    g = _gelu(y)
    z = _dot(g.astype(BF16), wglu_ref[...]) + bglu_ref[...]
    o_ref[:, BRANCH:2 * BRANCH] = (g * _sigmoid(z)).astype(BF16)
    for w in range(len(POOL_WINDOWS)):
        c0 = 3 * BRANCH + w * POOL_GROUP
        gq = p_ref[:, c0:c0 + POOL_GROUP]
        s = _dot(pm_ref[w], gq)
        yv = s * ic_ref[w] - gq.astype(F32)
        yc = _dot(yv.astype(BF16), pw_ref[w]) * ps_ref[:, w * POOL_GROUP:(w + 1) * POOL_GROUP]
        o0 = 2 * BRANCH + w * POOL_GROUP
        o_ref[:, o0:o0 + POOL_GROUP] = yc.astype(BF16)


def _pool_operators(tm, seg):
    t = jnp.arange(tm)
    base = (t // seg) * seg
    pos = t - base
    mats, invs = [], []
    for k in POOL_WINDOWS:
        lo = base + jnp.clip(pos - k // 2, 0, seg - 1)
        hi = base + jnp.clip(pos + k // 2 - 1, 0, seg - 1)
        mats.append(((t[None, :] >= lo[:, None]) & (t[None, :] <= hi[:, None])).astype(BF16))
        cnt = (hi - lo + 1).astype(F32)
        invs.append(jnp.broadcast_to((1.0 / cnt)[:, None], (tm, POOL_GROUP)))
    return jnp.stack(mats), jnp.stack(invs)


def _mixers(proj, ys, prm, seg):
    t, n = proj.shape
    tm = 256
    assert tm % seg == 0 and tm % CHUNK == 0
    pm, ic = _pool_operators(tm, seg)
    full = lambda a: pl.BlockSpec(a.shape, lambda i: (0,) * a.ndim)
    ws, bs, sd, wglu, bglu, pw, ps = (prm[k] for k in ('ws', 'bs', 'sd', 'wglu', 'bglu', 'pw', 'ps'))
    return pl.pallas_call(
        functools.partial(_mixers_kernel, tm=tm),
        out_shape=jax.ShapeDtypeStruct((t, 3 * BRANCH), BF16),
        grid=(t // tm,),
        in_specs=[pl.BlockSpec((tm, n), lambda i: (i, 0)),
                  pl.BlockSpec((tm, BRANCH), lambda i: (i, 0)),
                  full(ws), full(bs), full(sd), full(wglu), full(bglu),
                  full(pm), full(ic), full(pw), full(ps)],
        out_specs=pl.BlockSpec((tm, 3 * BRANCH), lambda i: (i, 0)),
        compiler_params=_cparams("parallel"),
        name="mixers",
    )(proj, ys, ws, bs, sd, wglu, bglu, pm, ic, pw, ps)


def _gate_merge_kernel(h_ref, y_ref, wg_ref, bg_ref, wb_ref, o_ref):
    h = h_ref[...]
    acc = None
    for k in range(3):
        gate = _sigmoid(_dot(h, wg_ref[k]) + bg_ref[k])
        term = gate * _dot(y_ref[:, k * BRANCH:(k + 1) * BRANCH], wb_ref[k])
        acc = term if acc is None else acc + term
    o_ref[...] = acc.astype(BF16)


def _gate_merge(h, yabc, wg, bg, wb):
    t, d = h.shape
    tm, tn = min(512, t), 512
    return pl.pallas_call(
        _gate_merge_kernel,
        out_shape=jax.ShapeDtypeStruct((t, d), BF16),
        grid=(d // tn, t // tm),
        in_specs=[pl.BlockSpec((tm, d), lambda j, i: (i, 0)),
                  pl.BlockSpec((tm, 3 * BRANCH), lambda j, i: (i, 0)),
                  pl.BlockSpec((3, d, tn), lambda j, i: (0, 0, j)),
                  pl.BlockSpec((3, 1, tn), lambda j, i: (0, 0, j)),
                  pl.BlockSpec((3, BRANCH, tn), lambda j, i: (0, 0, j))],
        out_specs=pl.BlockSpec((tm, tn), lambda j, i: (i, j)),
        compiler_params=_cparams("parallel", "parallel"),
        name="gate_merge",
    )(h, yabc, wg, bg, wb)


def _out_ln_kernel(m_ref, w_ref, x_ref, g_ref, lg_ref, lb_ref, sc_ref, sh_ref, wr_ref,
                   x1_ref, h2_ref, aff_ref, pre_ref, *, nj, tn, alpha, n_experts):
    j = pl.program_id(1)
    pre_ref[j] = alpha * x_ref[...] + g_ref[0] * _dot(m_ref[...], w_ref[...])

    @pl.when(j == nj - 1)
    def _():
        d = nj * tn
        tot = None
        for jj in range(nj):
            s = jnp.sum(pre_ref[jj], axis=-1, keepdims=True)
            tot = s if tot is None else tot + s
        mu = tot / d
        tot = None
        for jj in range(nj):
            dl = pre_ref[jj] - mu
            s = jnp.sum(dl * dl, axis=-1, keepdims=True)
            tot = s if tot is None else tot + s
        rstd = lax.rsqrt(tot / d + LN_EPS)
        for jj in range(nj):
            cs = slice(jj * tn, (jj + 1) * tn)
            y = (pre_ref[jj] - mu) * rstd * lg_ref[:, cs] + lb_ref[:, cs]
            x1_ref[:, cs] = y
            h2_ref[:, cs] = (y * (1.0 + sc_ref[0, :, cs]) + sh_ref[0, :, cs]).astype(BF16)
        logits = _dot(h2_ref[...], wr_ref[...])
        lane = lax.broadcasted_iota(jnp.int32, logits.shape, 1)
        logits = jnp.where(lane < n_experts, logits, -jnp.inf)
        e = jnp.exp(logits - jnp.max(logits, axis=-1, keepdims=True))
        aff_ref[...] = e / jnp.sum(e, axis=-1, keepdims=True)


def _out_ln(merged, w_out, x, gate, ln_g, ln_b, sc2, sh2, w_router, seq_len, alpha, n_experts):
    t, d = x.shape
    tm, tn = min(512, seq_len), 512
    per = seq_len // tm
    nj = d // tn
    row = pl.BlockSpec((1, d), lambda i, j: (0, 0))
    vec = pl.BlockSpec((1, 1, d), lambda i, j: (i // per, 0, 0))
    return pl.pallas_call(
        functools.partial(_out_ln_kernel, nj=nj, tn=tn, alpha=alpha, n_experts=n_experts),
        out_shape=(jax.ShapeDtypeStruct((t, d), F32),
                   jax.ShapeDtypeStruct((t, d), BF16),
                   jax.ShapeDtypeStruct((t, ROUTER_LANES), F32)),
        grid=(t // tm, nj),
        in_specs=[pl.BlockSpec((tm, d), lambda i, j: (i, 0)),
                  pl.BlockSpec((d, tn), lambda i, j: (0, j)),
                  pl.BlockSpec((tm, tn), lambda i, j: (i, j)),
                  pl.BlockSpec((1, 1, tn), lambda i, j: (i // per, 0, j)),
                  row, row, vec, vec,
                  pl.BlockSpec((d, ROUTER_LANES), lambda i, j: (0, 0))],
        out_specs=(pl.BlockSpec((tm, d), lambda i, j: (i, 0)),
                   pl.BlockSpec((tm, d), lambda i, j: (i, 0)),
                   pl.BlockSpec((tm, ROUTER_LANES), lambda i, j: (i, 0))),
        scratch_shapes=[pltpu.VMEM((nj, tm, tn), F32)],
        compiler_params=_cparams("parallel", "arbitrary"),
        name="out_ln",
    )(merged, w_out, x, gate, ln_g, ln_b, sc2, sh2, w_router)


def _ffn_kernel(x_ref, w1_ref, w3_ref, w2_ref, tw_ref, o_ref):
    x = x_ref[0]
    h1 = _dot(x, w1_ref[0])
    hid = (h1 * _sigmoid(h1)) * _dot(x, w3_ref[0])
    o_ref[0] = _dot(hid.astype(BF16), w2_ref[0]) * tw_ref[0]


def _expert_ffn(xs, w1, w3, w2, tw):
    e, m, d = xs.shape
    ff = w1.shape[2]
    tm = min(512, m)
    return pl.pallas_call(
        _ffn_kernel,
        out_shape=jax.ShapeDtypeStruct((e, m, d), F32),
        grid=(e, m // tm),
        in_specs=[pl.BlockSpec((1, tm, d), lambda ei, i: (ei, i, 0)),
                  pl.BlockSpec((1, d, ff), lambda ei, i: (ei, 0, 0)),
                  pl.BlockSpec((1, d, ff), lambda ei, i: (ei, 0, 0)),
                  pl.BlockSpec((1, ff, d), lambda ei, i: (ei, 0, 0)),
                  pl.BlockSpec((1, tm, 1), lambda ei, i: (ei, i, 0))],
        out_specs=pl.BlockSpec((1, tm, d), lambda ei, i: (ei, i, 0)),
        compiler_params=_cparams("parallel", "parallel"),
        name="expert_ffn",
    )(xs, w1, w3, w2, tw)


def _ln2_kernel(x_ref, f_ref, g_ref, lg_ref, lb_ref, sc_ref, sh_ref, x2_ref, h_ref, *, alpha):
    pre = alpha * x_ref[...] + g_ref[0] * f_ref[...]
    dl = pre - jnp.mean(pre, axis=-1, keepdims=True)
    var = jnp.mean(dl * dl, axis=-1, keepdims=True)
    y = dl * lax.rsqrt(var + LN_EPS) * lg_ref[...] + lb_ref[...]
    x2_ref[...] = y
    h_ref[...] = (y * (1.0 + sc_ref[0]) + sh_ref[0]).astype(BF16)


def _ln2(x, ffn, gate, ln_g, ln_b, sc, sh, seq_len, alpha):
    t, d = x.shape
    tm = min(256, seq_len)
    per = seq_len // tm
    tile = pl.BlockSpec((tm, d), lambda i: (i, 0))
    row = pl.BlockSpec((1, d), lambda i: (0, 0))
    vec = pl.BlockSpec((1, 1, d), lambda i: (i // per, 0, 0))
    return pl.pallas_call(
        functools.partial(_ln2_kernel, alpha=alpha),
        out_shape=(jax.ShapeDtypeStruct((t, d), F32), jax.ShapeDtypeStruct((t, d), BF16)),
        grid=(t // tm,),
        in_specs=[tile, tile, vec, row, row, vec, vec],
        out_specs=(tile, tile),
        compiler_params=_cparams("parallel"),
        name="ln2",
    )(x, ffn, gate, ln_g, ln_b, sc, sh)


def _mixer_inputs(h, lp, bs, ls, h0):
    proj = _proj(h, lp['w_in'])
    pb = proj[:, 2 * BRANCH:3 * BRANCH].reshape(bs, ls, BRANCH)
    ys, hfin = _s5(pb, h0, lp['s5_ops'])
    return proj, ys.reshape(bs * ls, BRANCH), hfin


def _expert_choice(h2, aff, lp, bs, ls):
    n_exp = lp['w1'].shape[0]
    d = h2.shape[1]
    cap = CAPACITY_FACTOR * ls // n_exp
    a = aff[:, :n_exp].reshape(bs, ls, n_exp)
    top_w, top_idx = lax.top_k(jnp.swapaxes(a, 1, 2), cap)
    gidx = top_idx + (jnp.arange(bs, dtype=top_idx.dtype) * ls)[:, None, None]
    gidx = gidx.transpose(1, 0, 2).reshape(-1)
    xs = jnp.take(h2, gidx, axis=0).reshape(n_exp, bs * cap, d)
    tw = top_w.transpose(1, 0, 2).reshape(n_exp, bs * cap, 1)
    oe = _expert_ffn(xs, lp['w1'], lp['w3'], lp['w2'], tw)
    return jnp.zeros((bs * ls, d), F32).at[gidx].add(oe.reshape(-1, d))


def _layer(x, h, mod, nxt, lp, bs, ls, seg, h0, alpha):
    proj, ys, hfin = _mixer_inputs(h, lp, bs, ls, h0)
    yabc = _mixers(proj, ys, lp, seg)
    merged = _gate_merge(h, yabc, lp['wg'], lp['bg'], lp['wb'])
    n_exp = lp['w1'].shape[0]
    x1, h2, aff = _out_ln(merged, lp['w_out'], x, mod[2], lp['ln1_g'], lp['ln1_b'], mod[4], mod[3],
                          lp['w_router'], ls, alpha, n_exp)
    ffn = _expert_choice(h2, aff, lp, bs, ls)
    x2, hn = _ln2(x1, ffn, mod[5], lp['ln2_g'], lp['ln2_b'], nxt[1], nxt[0], ls, alpha)
    return x2, hn, hfin


def kernel(x, c, ctx, c_ctx, ada_w_down, ada_w_up, ada_b, w_in, gmlp_w_s, gmlp_b_s, s5_a_re, s5_a_im, s5_log_step, s5_b_re, s5_b_im, s5_c_re, s5_c_im, s5_d, s5_w_glu, s5_b_glu, pool_w, pool_scale, w_gate, b_gate, w_branch, w_out, ln1_g, ln1_b, w_router, w1, w3, w2, ln2_g, ln2_b):
    bs, n, d = x.shape
    lc = ctx.shape[1]
    depth = w_in.shape[0]
    n_exp = w_router.shape[2]
    alpha = (2.0 * depth) ** 0.25
    assert bs + 1 <= SUBLANES and n % GRID_W == 0

    cond = jnp.zeros((SUBLANES, d), F32).at[:bs].set(c).at[bs].set(c_ctx)
    mods = _ada(cond, ada_w_down, ada_w_up, ada_b).reshape(depth, SUBLANES, N_MOD, d)

    def lat_mod(l):
        return mods[l, :bs].transpose(1, 0, 2)[:, :, None, :]

    def ctx_mod(l):
        return jnp.broadcast_to(mods[l, bs][:, None, None, :], (N_MOD, bs, 1, d))

    def layer_params(l):
        return dict(
            w_in=w_in[l].astype(BF16),
            s5_ops=_s5_operators(s5_a_re[l], s5_a_im[l], s5_log_step[l], s5_b_re[l], s5_b_im[l],
                                 s5_c_re[l], s5_c_im[l]),
            ws=gmlp_w_s[l].astype(BF16),
            bs=jnp.broadcast_to(gmlp_b_s[l][:, :, None], (HEADS, CHUNK, 128)).astype(F32),
            sd=s5_d[l].reshape(1, BRANCH), wglu=s5_w_glu[l].astype(BF16), bglu=s5_b_glu[l].reshape(1, BRANCH),
            pw=pool_w[l].astype(BF16), ps=pool_scale[l].reshape(1, BRANCH),
            wg=w_gate[l].reshape(d, 3, d).transpose(1, 0, 2).astype(BF16),
            bg=b_gate[l].reshape(3, 1, d), wb=w_branch[l].astype(BF16),
            w_out=w_out[l].astype(BF16), ln1_g=ln1_g[l].reshape(1, d), ln1_b=ln1_b[l].reshape(1, d),
            w_router=jnp.zeros((d, ROUTER_LANES), BF16).at[:, :n_exp].set(w_router[l].astype(BF16)),
            w1=w1[l].astype(BF16), w3=w3[l].astype(BF16), w2=w2[l].astype(BF16),
            ln2_g=ln2_g[l].reshape(1, d), ln2_b=ln2_b[l].reshape(1, d))

    xl = x.reshape(bs * n, d)
    xc = ctx.reshape(bs * lc, d)
    m0, c0 = lat_mod(0), ctx_mod(0)
    hl = _modulate(xl, m0[0], m0[1], n)
    hc = _modulate(xc, c0[0], c0[1], lc)
    zero_state = jnp.zeros((S5_GB, 2, SUBLANES, S5_GB * S5_STATE), F32)
    for l in range(depth):
        lp = layer_params(l)
        nl = min(l + 1, depth - 1)
        if l < depth - 1:
            xc, hc, ctx_final = _layer(xc, hc, ctx_mod(l), ctx_mod(nl), lp, bs, lc, lc, zero_state, alpha)
        else:
            _, _, ctx_final = _mixer_inputs(hc, lp, bs, lc, zero_state)
        xl, hl, _ = _layer(xl, hl, lat_mod(l), lat_mod(nl), lp, bs, n, GRID_W, ctx_final, alpha)
    return xl.reshape(bs, n, d)
```

```python
import functools
import math

import jax
import jax.numpy as jnp
from jax import lax
from jax.experimental import pallas as pl
from jax.experimental.pallas import tpu as pltpu

F32 = jnp.float32
BF16 = jnp.bfloat16

LN_EPS = 1e-6
GRID_W = 64
CHUNK = 128
BRANCH = 1024
HEADS = 8
S5_GROUP = 16
S5_STATE = 64
S5_GB = 8
POOL_WINDOWS = (2, 4, 8, 16)
POOL_GROUP = BRANCH // len(POOL_WINDOWS)
N_MOD = 6
ROUTER_LANES = 128
CAPACITY_FACTOR = 2
TAG_LANES = 128
TAG_BASE = 128
COMBINE_WINDOW = 64

V7X_VMEM_LIMIT = 60 * 1024 * 1024
SUBLANES = 8
BF16_ROW_TILE = 16


def _cparams(*sem):
    return pltpu.CompilerParams(dimension_semantics=sem, vmem_limit_bytes=V7X_VMEM_LIMIT)


def _sigmoid(x):
    return 1.0 / (1.0 + jnp.exp(-x))


def _gelu(x):
    c = math.sqrt(2.0 / math.pi)
    return x * (0.5 * (1.0 + jnp.tanh(c * (x + 0.044715 * (x * x * x)))))


def _dot(a, b):
    return jnp.dot(a, b, preferred_element_type=F32)


def _ada_down_kernel(c_ref, w_ref, o_ref):
    @pl.when(pl.program_id(1) == 0)
    def _():
        o_ref[...] = jnp.zeros_like(o_ref)

    c = c_ref[...]
    o_ref[0] += _dot((c * _sigmoid(c)).astype(BF16), w_ref[0].astype(BF16))


def _ada_up_kernel(r_ref, w_ref, b_ref, o_ref):
    o_ref[0] = _dot(r_ref[0].astype(BF16), w_ref[0].astype(BF16)) + b_ref[0]


def _ada(cond, w_down, w_up, b_up):
    depth, d, rank = w_down.shape
    nmod = w_up.shape[2]
    tk, tn = 1024, 2048
    r = pl.pallas_call(
        _ada_down_kernel,
        out_shape=jax.ShapeDtypeStruct((depth, SUBLANES, rank), F32),
        grid=(depth, d // tk),
        in_specs=[pl.BlockSpec((SUBLANES, tk), lambda l, k: (0, k)),
                  pl.BlockSpec((1, tk, rank), lambda l, k: (l, k, 0))],
        out_specs=pl.BlockSpec((1, SUBLANES, rank), lambda l, k: (l, 0, 0)),
        compiler_params=_cparams("parallel", "arbitrary"),
        name="ada_down",
    )(cond, w_down)
    return pl.pallas_call(
        _ada_up_kernel,
        out_shape=jax.ShapeDtypeStruct((depth, SUBLANES, nmod), F32),
        grid=(depth, nmod // tn),
        in_specs=[pl.BlockSpec((1, SUBLANES, rank), lambda l, n: (l, 0, 0)),
                  pl.BlockSpec((1, rank, tn), lambda l, n: (l, 0, n)),
                  pl.BlockSpec((1, 1, tn), lambda l, n: (l, 0, n))],
        out_specs=pl.BlockSpec((1, SUBLANES, tn), lambda l, n: (l, 0, n)),
        compiler_params=_cparams("parallel", "parallel"),
        name="ada_up",
    )(r, w_up, b_up.reshape(depth, 1, nmod))


def _modulate_kernel(x_ref, sh_ref, sc_ref, o_ref):
    o_ref[...] = (x_ref[...] * (1.0 + sc_ref[0]) + sh_ref[0]).astype(BF16)


def _modulate(x, sh, sc, seq_len):
    t, d = x.shape
    tm = min(512, seq_len)
    per = seq_len // tm
    vec = pl.BlockSpec((1, 1, d), lambda i: (i // per, 0, 0))
    return pl.pallas_call(
        _modulate_kernel,
        out_shape=jax.ShapeDtypeStruct((t, d), BF16),
        grid=(t // tm,),
        in_specs=[pl.BlockSpec((tm, d), lambda i: (i, 0)), vec, vec],
        out_specs=pl.BlockSpec((tm, d), lambda i: (i, 0)),
        compiler_params=_cparams("parallel"),
        name="modulate",
    )(x, sh, sc)


def _proj_kernel(a_ref, w_ref, o_ref):
    j = pl.program_id(0)
    acc = _dot(a_ref[...], w_ref[...])

    @pl.when(j == 0)
    def _():
        o_ref[...] = _gelu(acc).astype(BF16)

    @pl.when(j == 1)
    def _():
        g = _gelu(acc)
        dlt = g - jnp.mean(g, axis=-1, keepdims=True)
        var = jnp.mean(dlt * dlt, axis=-1, keepdims=True)
        o_ref[...] = (dlt * lax.rsqrt(var + LN_EPS)).astype(BF16)

    @pl.when(j >= 2)
    def _():
        o_ref[...] = acc.astype(BF16)


def _proj(h, w_in):
    t, d = h.shape
    n = w_in.shape[1]
    tm = min(512, t)
    return pl.pallas_call(
        _proj_kernel,
        out_shape=jax.ShapeDtypeStruct((t, n), BF16),
        grid=(n // BRANCH, t // tm),
        in_specs=[pl.BlockSpec((tm, d), lambda j, i: (i, 0)),
                  pl.BlockSpec((d, BRANCH), lambda j, i: (0, j))],
        out_specs=pl.BlockSpec((tm, BRANCH), lambda j, i: (i, j)),
        compiler_params=_cparams("parallel", "parallel"),
        name="proj",
    )(h, w_in)


def _s5_kernel(uf_ref, ub_ref, pf_ref, pb_ref, pft_ref, pbt_ref, bm_ref, cm_ref, lam_ref, h0_ref,
               yf_ref, yb_ref, hfin_ref, bu_ref, st_ref, *, tblk, nseq):
    i = pl.program_id(0)
    half = S5_GB * S5_STATE
    src_rows = nseq * tblk

    @pl.when(i == 0)
    def _():
        st_ref[...] = h0_ref[...]

    for gb in range(S5_GB):
        cols = slice(gb * 128, (gb + 1) * 128)
        uf = uf_ref[:, :, cols].reshape(src_rows, 128)
        ub = ub_ref[:, :, cols].reshape(src_rows, 128)
        lhs = jnp.concatenate([_dot(pf_ref[...], uf), _dot(pb_ref[...], ub)], axis=1).astype(BF16)
        bu_ref[...] = _dot(lhs, bm_ref[gb])
        lr = lam_ref[gb, 0]
        li = lam_ref[gb, 1]

        def step(t, carry):
            hr, hi = carry
            r0 = pl.multiple_of(t * SUBLANES, SUBLANES)
            nr = lr * hr - li * hi + bu_ref[pl.ds(r0, SUBLANES), 0:half]
            ni = lr * hi + li * hr + bu_ref[pl.ds(r0, SUBLANES), half:2 * half]
            bu_ref[pl.ds(r0, SUBLANES), 0:half] = nr
            bu_ref[pl.ds(r0, SUBLANES), half:2 * half] = ni
            return nr, ni

        hr, hi = lax.fori_loop(0, tblk, step, (st_ref[gb, 0], st_ref[gb, 1]), unroll=8)
        st_ref[gb, 0] = hr
        st_ref[gb, 1] = hi
        yy = _dot(bu_ref[...].astype(BF16), cm_ref[gb]).astype(BF16)
        yf = _dot(pft_ref[...], yy[:, :128]).astype(BF16)
        yb = _dot(pbt_ref[...], yy[:, 128:]).astype(BF16)
        yf_ref[:, :, cols] = yf.reshape(nseq, tblk, 128)
        yb_ref[:, :, cols] = yb.reshape(nseq, tblk, 128)

    @pl.when(i == pl.num_programs(0) - 1)
    def _():
        hfin_ref[...] = st_ref[...]


def _s5_operators(a_re, a_im, log_step, b_re, b_im, c_re, c_im):
    a_re = a_re.astype(F32)
    a_im = a_im.astype(F32)
    step = jnp.exp(log_step.astype(F32))[..., None]
    mag = jnp.exp(a_re * step)
    lam_re = mag * jnp.cos(a_im * step)
    lam_im = mag * jnp.sin(a_im * step)
    den = a_re * a_re + a_im * a_im
    f_re = ((lam_re - 1.0) * a_re + lam_im * a_im) / den
    f_im = (lam_im * a_re - (lam_re - 1.0) * a_im) / den
    b_re = b_re.astype(F32)[None]
    b_im = b_im.astype(F32)[None]
    bb_re = f_re[..., None] * b_re - f_im[..., None] * b_im
    bb_im = f_re[..., None] * b_im + f_im[..., None] * b_re
    ngl = 128 // S5_GROUP
    eye = jnp.eye(ngl, dtype=F32)
    bb = jnp.stack([bb_re, bb_im], axis=1)
    bb = bb.reshape(2, 2, S5_GB, ngl, S5_STATE, S5_GROUP)
    bmat = jnp.einsum('drgaph,ab->gdahrbp', bb, eye).reshape(S5_GB, 2 * 128, 2 * ngl * S5_STATE)
    cc = jnp.stack([c_re.astype(F32), -c_im.astype(F32)], axis=1)
    cc = cc.reshape(2, 2, S5_GB, ngl, S5_GROUP, S5_STATE)
    cmat = jnp.einsum('drgahp,ab->grapdbh', cc, eye).reshape(S5_GB, 2 * ngl * S5_STATE, 2 * 128)
    lam = jnp.stack([lam_re, lam_im], axis=1)
    lam = lam.reshape(2, 2, S5_GB, ngl * S5_STATE).transpose(2, 1, 0, 3)
    lam = jnp.repeat(lam, SUBLANES // 2, axis=2)
    return bmat.astype(BF16), cmat.astype(BF16), lam


def _scan_order(nseq, tblk):
    step = jnp.arange(tblk * SUBLANES) // SUBLANES
    seq = jnp.arange(tblk * SUBLANES) % SUBLANES
    src = jnp.arange(nseq * tblk)[None, :]
    fwd = (seq < nseq)[:, None] & (src == (seq * tblk + step)[:, None])
    bwd = (seq >= nseq)[:, None] & (src == ((seq - nseq) * tblk + (tblk - 1 - step))[:, None])
    return fwd.astype(BF16), bwd.astype(BF16), fwd.T.astype(BF16), bwd.T.astype(BF16)


def _s5(proj, h0, ops, col_block):
    bs, ls, _ = proj.shape
    assert 2 * bs == SUBLANES
    bmat, cmat, lam = ops
    tblk = 64
    nblk = ls // tblk
    rows = tblk * SUBLANES
    perms = _scan_order(bs, tblk)
    st_shape = (S5_GB, 2, SUBLANES, S5_GB * S5_STATE)
    full = lambda shape: pl.BlockSpec(shape, lambda i: (0,) * len(shape))
    y_shape = jax.ShapeDtypeStruct((bs, ls, BRANCH), BF16)
    return pl.pallas_call(
        functools.partial(_s5_kernel, tblk=tblk, nseq=bs),
        out_shape=(y_shape, y_shape, jax.ShapeDtypeStruct(st_shape, F32)),
        grid=(nblk,),
        in_specs=[pl.BlockSpec((bs, tblk, BRANCH), lambda i: (0, i, col_block)),
                  pl.BlockSpec((bs, tblk, BRANCH), lambda i: (0, nblk - 1 - i, col_block))]
                 + [full(p.shape) for p in perms]
                 + [full(bmat.shape), full(cmat.shape), full(lam.shape), full(st_shape)],
        out_specs=(pl.BlockSpec((bs, tblk, BRANCH), lambda i: (0, i, 0)),
                   pl.BlockSpec((bs, tblk, BRANCH), lambda i: (0, nblk - 1 - i, 0)),
                   full(st_shape)),
        scratch_shapes=[pltpu.VMEM((rows, 2 * S5_GB * S5_STATE), F32), pltpu.VMEM(st_shape, F32)],
        compiler_params=_cparams("arbitrary"),
        name="s5_scan",
    )(proj, proj, *perms, bmat, cmat, lam, h0)


def _mixers_kernel(p_ref, yf_ref, yb_ref, ws_ref, bs_ref, sd_ref, wglu_ref, bglu_ref,
                   pm_ref, ic_ref, pw_ref, ps_ref, o_ref, *, tm):
    for ch in range(tm // CHUNK):
        r = slice(ch * CHUNK, (ch + 1) * CHUNK)
        for h in range(HEADS):
            vn = p_ref[r, BRANCH + h * 128:BRANCH + (h + 1) * 128]
            mixed = _dot(ws_ref[h], vn) + bs_ref[h]
            o_ref[r, h * 128:(h + 1) * 128] = (p_ref[r, h * 128:(h + 1) * 128].astype(F32) * mixed).astype(BF16)
    y = (yf_ref[...].astype(F32) + yb_ref[...].astype(F32)
         + p_ref[:, 2 * BRANCH:3 * BRANCH].astype(F32) * sd_ref[...])
    g = _gelu(y)
    z = _dot(g.astype(BF16), wglu_ref[...]) + bglu_ref[...]
    o_ref[:, BRANCH:2 * BRANCH] = (g * _sigmoid(z)).astype(BF16)
    for w in range(len(POOL_WINDOWS)):
        c0 = 3 * BRANCH + w * POOL_GROUP
        gq = p_ref[:, c0:c0 + POOL_GROUP]
        s = _dot(pm_ref[w], gq)
        yv = s * ic_ref[w] - gq.astype(F32)
        yc = _dot(yv.astype(BF16), pw_ref[w]) * ps_ref[:, w * POOL_GROUP:(w + 1) * POOL_GROUP]
        o0 = 2 * BRANCH + w * POOL_GROUP
        o_ref[:, o0:o0 + POOL_GROUP] = yc.astype(BF16)


def _pool_operators(tm, seg):
    t = jnp.arange(tm)
    base = (t // seg) * seg
    pos = t - base
    mats, invs = [], []
    for k in POOL_WINDOWS:
        lo = base + jnp.clip(pos - k // 2, 0, seg - 1)
        hi = base + jnp.clip(pos + k // 2 - 1, 0, seg - 1)
        mats.append(((t[None, :] >= lo[:, None]) & (t[None, :] <= hi[:, None])).astype(BF16))
        cnt = (hi - lo + 1).astype(F32)
        invs.append(jnp.broadcast_to((1.0 / cnt)[:, None], (tm, POOL_GROUP)))
    return jnp.stack(mats), jnp.stack(invs)


def _mixers(proj, yf, yb, prm, seg):
    t, n = proj.shape
    tm = 256
    assert tm % seg == 0 and tm % CHUNK == 0
    pm, ic = _pool_operators(tm, seg)
    full = lambda a: pl.BlockSpec(a.shape, lambda i: (0,) * a.ndim)
    ws, bs, sd, wglu, bglu, pw, ps = (prm[k] for k in ('ws', 'bs', 'sd', 'wglu', 'bglu', 'pw', 'ps'))
    return pl.pallas_call(
        functools.partial(_mixers_kernel, tm=tm),
        out_shape=jax.ShapeDtypeStruct((t, 3 * BRANCH), BF16),
        grid=(t // tm,),
        in_specs=[pl.BlockSpec((tm, n), lambda i: (i, 0)),
                  pl.BlockSpec((tm, BRANCH), lambda i: (i, 0)),
                  pl.BlockSpec((tm, BRANCH), lambda i: (i, 0)),
                  full(ws), full(bs), full(sd), full(wglu), full(bglu),
                  full(pm), full(ic), full(pw), full(ps)],
        out_specs=pl.BlockSpec((tm, 3 * BRANCH), lambda i: (i, 0)),
        compiler_params=_cparams("parallel"),
        name="mixers",
    )(proj, yf, yb, ws, bs, sd, wglu, bglu, pm, ic, pw, ps)


def _gate_merge_kernel(h_ref, y_ref, wg_ref, bg_ref, wb_ref, o_ref):
    h = h_ref[...]
    acc = None
    for k in range(3):
        gate = _sigmoid(_dot(h, wg_ref[k]) + bg_ref[k])
        term = gate * _dot(y_ref[:, k * BRANCH:(k + 1) * BRANCH], wb_ref[k])
        acc = term if acc is None else acc + term
    o_ref[...] = acc.astype(BF16)


def _gate_merge(h, yabc, wg, bg, wb):
    t, d = h.shape
    tm, tn = min(512, t), 512
    return pl.pallas_call(
        _gate_merge_kernel,
        out_shape=jax.ShapeDtypeStruct((t, d), BF16),
        grid=(d // tn, t // tm),
        in_specs=[pl.BlockSpec((tm, d), lambda j, i: (i, 0)),
                  pl.BlockSpec((tm, 3 * BRANCH), lambda j, i: (i, 0)),
                  pl.BlockSpec((3, d, tn), lambda j, i: (0, 0, j)),
                  pl.BlockSpec((3, 1, tn), lambda j, i: (0, 0, j)),
                  pl.BlockSpec((3, BRANCH, tn), lambda j, i: (0, 0, j))],
        out_specs=pl.BlockSpec((tm, tn), lambda j, i: (i, j)),
        compiler_params=_cparams("parallel", "parallel"),
        name="gate_merge",
    )(h, yabc, wg, bg, wb)


def _out_ln_kernel(m_ref, w_ref, x_ref, g_ref, lg_ref, lb_ref, sc_ref, sh_ref, wr_ref,
                   x1_ref, h2_ref, aff_ref, pre_ref, *, nj, tn, alpha, n_experts):
    j = pl.program_id(1)
    pre_ref[j] = alpha * x_ref[...] + g_ref[0] * _dot(m_ref[...], w_ref[...])

    @pl.when(j == nj - 1)
    def _():
        d = nj * tn
        tot = None
        for jj in range(nj):
            s = jnp.sum(pre_ref[jj], axis=-1, keepdims=True)
            tot = s if tot is None else tot + s
        mu = tot / d
        tot = None
        for jj in range(nj):
            dl = pre_ref[jj] - mu
            s = jnp.sum(dl * dl, axis=-1, keepdims=True)
            tot = s if tot is None else tot + s
        rstd = lax.rsqrt(tot / d + LN_EPS)
        for jj in range(nj):
            cs = slice(jj * tn, (jj + 1) * tn)
            y = (pre_ref[jj] - mu) * rstd * lg_ref[:, cs] + lb_ref[:, cs]
            x1_ref[:, cs] = y
            h2_ref[:, cs] = (y * (1.0 + sc_ref[0, :, cs]) + sh_ref[0, :, cs]).astype(BF16)
        logits = _dot(h2_ref[...], wr_ref[...])
        lane = lax.broadcasted_iota(jnp.int32, logits.shape, 1)
        logits = jnp.where(lane < n_experts, logits, -jnp.inf)
        e = jnp.exp(logits - jnp.max(logits, axis=-1, keepdims=True))
        aff_ref[...] = e / jnp.sum(e, axis=-1, keepdims=True)


def _out_ln(merged, w_out, x, gate, ln_g, ln_b, sc2, sh2, w_router, seq_len, alpha, n_experts):
    t, d = x.shape
    tm, tn = min(512, seq_len), 512
    per = seq_len // tm
    nj = d // tn
    row = pl.BlockSpec((1, d), lambda i, j: (0, 0))
    vec = pl.BlockSpec((1, 1, d), lambda i, j: (i // per, 0, 0))
    return pl.pallas_call(
        functools.partial(_out_ln_kernel, nj=nj, tn=tn, alpha=alpha, n_experts=n_experts),
        out_shape=(jax.ShapeDtypeStruct((t, d), F32),
                   jax.ShapeDtypeStruct((t, d), BF16),
                   jax.ShapeDtypeStruct((t, ROUTER_LANES), F32)),
        grid=(t // tm, nj),
        in_specs=[pl.BlockSpec((tm, d), lambda i, j: (i, 0)),
                  pl.BlockSpec((d, tn), lambda i, j: (0, j)),
                  pl.BlockSpec((tm, tn), lambda i, j: (i, j)),
                  pl.BlockSpec((1, 1, tn), lambda i, j: (i // per, 0, j)),
                  row, row, vec, vec,
                  pl.BlockSpec((d, ROUTER_LANES), lambda i, j: (0, 0))],
        out_specs=(pl.BlockSpec((tm, d), lambda i, j: (i, 0)),
                   pl.BlockSpec((tm, d), lambda i, j: (i, 0)),
                   pl.BlockSpec((tm, ROUTER_LANES), lambda i, j: (i, 0))),
        scratch_shapes=[pltpu.VMEM((nj, tm, tn), F32)],
        compiler_params=_cparams("parallel", "arbitrary"),
        name="out_ln",
    )(merged, w_out, x, gate, ln_g, ln_b, sc2, sh2, w_router)


def _ffn_kernel(x_ref, w1_ref, w3_ref, w2_ref, tw_ref, tag_ref, o_ref):
    d = x_ref.shape[2]
    x = x_ref[0]
    h1 = _dot(x, w1_ref[0])
    hid = (h1 * _sigmoid(h1)) * _dot(x, w3_ref[0])
    o_ref[0, :, :d] = (_dot(hid.astype(BF16), w2_ref[0]) * tw_ref[0]).astype(BF16)
    o_ref[0, :, d:] = tag_ref[0]


def _expert_ffn(xs, w1, w3, w2, tw, tag):
    e, m, d = xs.shape
    ff = w1.shape[2]
    tm = min(512, m)
    return pl.pallas_call(
        _ffn_kernel,
        out_shape=jax.ShapeDtypeStruct((e, m, d + TAG_LANES), BF16),
        grid=(e, m // tm),
        in_specs=[pl.BlockSpec((1, tm, d), lambda ei, i: (ei, i, 0)),
                  pl.BlockSpec((1, d, ff), lambda ei, i: (ei, 0, 0)),
                  pl.BlockSpec((1, d, ff), lambda ei, i: (ei, 0, 0)),
                  pl.BlockSpec((1, ff, d), lambda ei, i: (ei, 0, 0)),
                  pl.BlockSpec((1, tm, 1), lambda ei, i: (ei, i, 0)),
                  pl.BlockSpec((1, tm, TAG_LANES), lambda ei, i: (ei, i, 0))],
        out_specs=pl.BlockSpec((1, tm, d + TAG_LANES), lambda ei, i: (ei, i, 0)),
        compiler_params=_cparams("parallel", "parallel"),
        name="expert_ffn",
    )(xs, w1, w3, w2, tw, tag)


def _tag_tokens(tags):
    lane = lax.broadcasted_iota(jnp.int32, (SUBLANES, TAG_LANES), 1)
    coef = jnp.where(lane == 0, float(TAG_BASE), jnp.where(lane == 1, 1.0, 0.0)).astype(BF16)
    return lax.dot_general(coef, tags, (((1,), (1,)), ((), ())), preferred_element_type=F32)[0:1]


def _select_rows(tok, valid, first_token, tm):
    ids = (first_token + lax.broadcasted_iota(jnp.int32, (tm, tok.shape[1]), 0)).astype(F32)
    return jnp.where(jnp.where(valid, tok, -1.0) == ids, 1.0, 0.0).astype(BF16)


def _combine_ln2_kernel(s0_ref, cnt_ref, x_ref, oe_ref, g_ref, lg_ref, lb_ref, sc_ref, sh_ref,
                        x2_ref, h_ref, win_ref, xw_ref, acc_ref, sem, xsem,
                        *, alpha, n_exp, win, tm, total_rows):
    i = pl.program_id(0)
    d = x_ref.shape[1]
    slot = i % 2
    last_start = total_rows - win
    cover = win - BF16_ROW_TILE
    first_token = i * tm

    def window_start(first_row):
        start = (jnp.minimum(first_row, last_start) // BF16_ROW_TILE) * BF16_ROW_TILE
        return pl.multiple_of(start, BF16_ROW_TILE)

    def window_copy(start, buf, e):
        return pltpu.make_async_copy(oe_ref.at[pl.ds(start, win)],
                                     win_ref.at[buf, pl.ds(e * win, win)], sem.at[buf, e])

    def fetch(tile, buf):
        for e in range(n_exp):
            window_copy(window_start(s0_ref[tile * n_exp + e]), buf, e).start()

    @pl.when(i == 0)
    def _():
        fetch(0, 0)

    @pl.when(i + 1 < pl.num_programs(0))
    def _():
        fetch(i + 1, 1 - slot)

    for e in range(n_exp):
        window_copy(0, slot, e).wait()

    wl = lax.broadcasted_iota(jnp.int32, (1, n_exp * win), 1)
    row = jnp.zeros_like(wl)
    lo = jnp.zeros_like(wl)
    hi = jnp.zeros_like(wl)
    for e in range(n_exp):
        s0 = s0_ref[i * n_exp + e]
        cnt = cnt_ref[i * n_exp + e]
        in_e = (wl >= e * win) & (wl < (e + 1) * win)
        row = jnp.where(in_e, window_start(s0) + (wl - e * win), row)
        lo = jnp.where(in_e, s0, lo)
        hi = jnp.where(in_e, s0 + jnp.minimum(cnt, cover), hi)
    valid = (row >= lo) & (row < hi)
    tok = _tag_tokens(win_ref[slot, :, d:])
    acc_ref[...] = _dot(_select_rows(tok, valid, first_token, tm), win_ref[slot, :, :d])

    for e in range(n_exp):
        s0 = s0_ref[i * n_exp + e]
        cnt = cnt_ref[i * n_exp + e]

        def extra(k, carry, s0=s0, cnt=cnt):
            lo_k = s0 + k * cover
            start = window_start(lo_k)
            cp = pltpu.make_async_copy(oe_ref.at[pl.ds(start, win)], xw_ref, xsem.at[0])
            cp.start()
            cp.wait()
            row_k = start + lax.broadcasted_iota(jnp.int32, (1, win), 1)
            ok = (row_k >= lo_k) & (row_k < jnp.minimum(lo_k + cover, s0 + cnt))
            sel = _select_rows(_tag_tokens(xw_ref[:, d:]), ok, first_token, tm)
            acc_ref[...] += _dot(sel, xw_ref[:, :d])
            return carry

        lax.fori_loop(1, (cnt + cover - 1) // cover, extra, 0)

    pre = alpha * x_ref[...] + g_ref[0] * acc_ref[...]
    dl = pre - jnp.mean(pre, axis=-1, keepdims=True)
    var = jnp.mean(dl * dl, axis=-1, keepdims=True)
    y = dl * lax.rsqrt(var + LN_EPS) * lg_ref[...] + lb_ref[...]
    x2_ref[...] = y
    h_ref[...] = (y * (1.0 + sc_ref[0]) + sh_ref[0]).astype(BF16)


def _combine_ln2(x, oe, s0, cnt, gate, ln_g, ln_b, sc, sh, seq_len, tm, alpha, n_exp):
    t, d = x.shape
    total_rows = oe.shape[0]
    win = COMBINE_WINDOW
    assert total_rows >= win and total_rows % BF16_ROW_TILE == 0 and t <= TAG_BASE * 256
    per = seq_len // tm
    tile = pl.BlockSpec((tm, d), lambda i, *_: (i, 0))
    row = pl.BlockSpec((1, d), lambda i, *_: (0, 0))
    vec = pl.BlockSpec((1, 1, d), lambda i, *_: (i // per, 0, 0))
    return pl.pallas_call(
        functools.partial(_combine_ln2_kernel, alpha=alpha, n_exp=n_exp, win=win, tm=tm,
                          total_rows=total_rows),
        out_shape=(jax.ShapeDtypeStruct((t, d), F32), jax.ShapeDtypeStruct((t, d), BF16)),
        grid_spec=pltpu.PrefetchScalarGridSpec(
            num_scalar_prefetch=2,
            grid=(t // tm,),
            in_specs=[tile, pl.BlockSpec(memory_space=pl.ANY), vec, row, row, vec, vec],
            out_specs=(tile, tile),
            scratch_shapes=[pltpu.VMEM((2, n_exp * win, d + TAG_LANES), BF16),
                            pltpu.VMEM((win, d + TAG_LANES), BF16),
                            pltpu.VMEM((tm, d), F32),
                            pltpu.SemaphoreType.DMA((2, n_exp)),
                            pltpu.SemaphoreType.DMA((1,))]),
        compiler_params=_cparams("arbitrary"),
        name="combine_ln2",
    )(s0, cnt, x, oe, gate, ln_g, ln_b, sc, sh)


def _mixer_inputs(h, lp, bs, ls, h0):
    proj = _proj(h, lp['w_in'])
    yf, yb, hfin = _s5(proj.reshape(bs, ls, -1), h0, lp['s5_ops'], col_block=2)
    return proj, yf.reshape(bs * ls, BRANCH), yb.reshape(bs * ls, BRANCH), hfin


def _expert_choice(h2, aff, lp, bs, ls, tm):
    n_exp = lp['w1'].shape[0]
    d = h2.shape[1]
    cap = CAPACITY_FACTOR * ls // n_exp
    m = bs * cap
    a = aff[:, :n_exp].reshape(bs, ls, n_exp)
    top_w, top_idx = lax.top_k(jnp.swapaxes(a, 1, 2), cap)
    top_idx, top_w = lax.sort((top_idx, top_w), dimension=2, num_keys=1)
    gidx = top_idx + (jnp.arange(bs, dtype=top_idx.dtype) * ls)[:, None, None]
    gidx = gidx.transpose(1, 0, 2).reshape(-1)
    xs = jnp.take(h2, gidx, axis=0).reshape(n_exp, m, d)
    tw = top_w.transpose(1, 0, 2).reshape(n_exp, m, 1)
    tag = jnp.stack([gidx // TAG_BASE, gidx % TAG_BASE], axis=-1).astype(BF16)
    tag = jnp.pad(tag, ((0, 0), (0, TAG_LANES - 2))).reshape(n_exp, m, TAG_LANES)
    oe = _expert_ffn(xs, lp['w1'], lp['w3'], lp['w2'], tw, tag)
    tile_start = jnp.arange(ls // tm, dtype=top_idx.dtype) * tm
    below = jnp.sum(top_idx[:, :, None, :] < tile_start[None, None, :, None], axis=-1)
    upto = jnp.concatenate([below[:, :, 1:], jnp.full((bs, n_exp, 1), cap, below.dtype)], axis=2)
    base = (jnp.arange(n_exp) * m)[None, :, None] + (jnp.arange(bs) * cap)[:, None, None]
    s0 = (base + below).transpose(0, 2, 1).reshape(-1).astype(jnp.int32)
    cnt = (upto - below).transpose(0, 2, 1).reshape(-1).astype(jnp.int32)
    return oe.reshape(n_exp * m, d + TAG_LANES), s0, cnt


def _layer(x, h, mod, nxt, lp, bs, ls, seg, h0, alpha):
    proj, yf, yb, hfin = _mixer_inputs(h, lp, bs, ls, h0)
    yabc = _mixers(proj, yf, yb, lp, seg)
    merged = _gate_merge(h, yabc, lp['wg'], lp['bg'], lp['wb'])
    n_exp = lp['w1'].shape[0]
    x1, h2, aff = _out_ln(merged, lp['w_out'], x, mod[2], lp['ln1_g'], lp['ln1_b'], mod[4], mod[3],
                          lp['w_router'], ls, alpha, n_exp)
    tm = min(256, ls)
    oe, s0, cnt = _expert_choice(h2, aff, lp, bs, ls, tm)
    x2, hn = _combine_ln2(x1, oe, s0, cnt, mod[5], lp['ln2_g'], lp['ln2_b'], nxt[1], nxt[0],
                          ls, tm, alpha, n_exp)
    return x2, hn, hfin


def kernel(x, c, ctx, c_ctx, ada_w_down, ada_w_up, ada_b, w_in, gmlp_w_s, gmlp_b_s, s5_a_re, s5_a_im, s5_log_step, s5_b_re, s5_b_im, s5_c_re, s5_c_im, s5_d, s5_w_glu, s5_b_glu, pool_w, pool_scale, w_gate, b_gate, w_branch, w_out, ln1_g, ln1_b, w_router, w1, w3, w2, ln2_g, ln2_b):
    bs, n, d = x.shape
    lc = ctx.shape[1]
    depth = w_in.shape[0]
    n_exp = w_router.shape[2]
    alpha = (2.0 * depth) ** 0.25
    assert bs + 1 <= SUBLANES and n % GRID_W == 0

    cond = jnp.zeros((SUBLANES, d), F32).at[:bs].set(c).at[bs].set(c_ctx)
    mods = _ada(cond, ada_w_down, ada_w_up, ada_b).reshape(depth, SUBLANES, N_MOD, d)

    def lat_mod(l):
        return mods[l, :bs].transpose(1, 0, 2)[:, :, None, :]

    def ctx_mod(l):
        return jnp.broadcast_to(mods[l, bs][:, None, None, :], (N_MOD, bs, 1, d))

    def layer_params(l):
        return dict(
            w_in=w_in[l].astype(BF16),
            s5_ops=_s5_operators(s5_a_re[l], s5_a_im[l], s5_log_step[l], s5_b_re[l], s5_b_im[l],
                                 s5_c_re[l], s5_c_im[l]),
            ws=gmlp_w_s[l].astype(BF16),
            bs=jnp.broadcast_to(gmlp_b_s[l][:, :, None], (HEADS, CHUNK, 128)).astype(F32),
            sd=s5_d[l].reshape(1, BRANCH), wglu=s5_w_glu[l].astype(BF16), bglu=s5_b_glu[l].reshape(1, BRANCH),
            pw=pool_w[l].astype(BF16), ps=pool_scale[l].reshape(1, BRANCH),
            wg=w_gate[l].reshape(d, 3, d).transpose(1, 0, 2).astype(BF16),
            bg=b_gate[l].reshape(3, 1, d), wb=w_branch[l].astype(BF16),
            w_out=w_out[l].astype(BF16), ln1_g=ln1_g[l].reshape(1, d), ln1_b=ln1_b[l].reshape(1, d),
            w_router=jnp.zeros((d, ROUTER_LANES), BF16).at[:, :n_exp].set(w_router[l].astype(BF16)),
            w1=w1[l].astype(BF16), w3=w3[l].astype(BF16), w2=w2[l].astype(BF16),
            ln2_g=ln2_g[l].reshape(1, d), ln2_b=ln2_b[l].reshape(1, d))

    xl = x.reshape(bs * n, d)
    xc = ctx.reshape(bs * lc, d)
    m0, c0 = lat_mod(0), ctx_mod(0)
    hl = _modulate(xl, m0[0], m0[1], n)
    hc = _modulate(xc, c0[0], c0[1], lc)
    zero_state = jnp.zeros((S5_GB, 2, SUBLANES, S5_GB * S5_STATE), F32)
    for l in range(depth):
        lp = layer_params(l)
        nl = min(l + 1, depth - 1)
        if l < depth - 1:
            xc, hc, ctx_final = _layer(xc, hc, ctx_mod(l), ctx_mod(nl), lp, bs, lc, lc, zero_state, alpha)
        else:
            ctx_final = _mixer_inputs(hc, lp, bs, lc, zero_state)[-1]
        xl, hl, _ = _layer(xl, hl, lat_mod(l), lat_mod(nl), lp, bs, n, GRID_W, ctx_final, alpha)
    return xl.reshape(bs, n, d)
```

```python
import functools
import math

import jax
import jax.numpy as jnp
from jax import lax
from jax.experimental import pallas as pl
from jax.experimental.pallas import tpu as pltpu

F32 = jnp.float32
BF16 = jnp.bfloat16

LN_EPS = 1e-6
GRID_W = 64
CHUNK = 128
BRANCH = 1024
HEADS = 8
S5_GROUP = 16
S5_STATE = 64
S5_GB = 8
POOL_WINDOWS = (2, 4, 8, 16)
POOL_GROUP = BRANCH // len(POOL_WINDOWS)
N_MOD = 6
ROUTER_LANES = 128
CAPACITY_FACTOR = 2
TAG_LANES = 128
TAG_BASE = 128
COMBINE_WINDOW = 64

V7X_VMEM_LIMIT = 60 * 1024 * 1024
SUBLANES = 8
BF16_ROW_TILE = 16


def _cparams(*sem):
    return pltpu.CompilerParams(dimension_semantics=sem, vmem_limit_bytes=V7X_VMEM_LIMIT)


def _sigmoid(x):
    return 1.0 / (1.0 + jnp.exp(-x))


def _gelu(x):
    c = math.sqrt(2.0 / math.pi)
    return x * (0.5 * (1.0 + jnp.tanh(c * (x + 0.044715 * (x * x * x)))))


def _dot(a, b):
    return jnp.dot(a, b, preferred_element_type=F32)


def _ada_down_kernel(c_ref, w_ref, o_ref):
    @pl.when(pl.program_id(1) == 0)
    def _():
        o_ref[...] = jnp.zeros_like(o_ref)

    c = c_ref[...]
    o_ref[0] += _dot((c * _sigmoid(c)).astype(BF16), w_ref[0].astype(BF16))


def _ada_up_kernel(r_ref, w_ref, b_ref, o_ref):
    o_ref[0] = _dot(r_ref[0].astype(BF16), w_ref[0].astype(BF16)) + b_ref[0]


def _ada(cond, w_down, w_up, b_up):
    depth, d, rank = w_down.shape
    nmod = w_up.shape[2]
    tk, tn = 1024, 2048
    r = pl.pallas_call(
        _ada_down_kernel,
        out_shape=jax.ShapeDtypeStruct((depth, SUBLANES, rank), F32),
        grid=(depth, d // tk),
        in_specs=[pl.BlockSpec((SUBLANES, tk), lambda l, k: (0, k)),
                  pl.BlockSpec((1, tk, rank), lambda l, k: (l, k, 0))],
        out_specs=pl.BlockSpec((1, SUBLANES, rank), lambda l, k: (l, 0, 0)),
        compiler_params=_cparams("parallel", "arbitrary"),
        name="ada_down",
    )(cond, w_down)
    return pl.pallas_call(
        _ada_up_kernel,
        out_shape=jax.ShapeDtypeStruct((depth, SUBLANES, nmod), F32),
        grid=(depth, nmod // tn),
        in_specs=[pl.BlockSpec((1, SUBLANES, rank), lambda l, n: (l, 0, 0)),
                  pl.BlockSpec((1, rank, tn), lambda l, n: (l, 0, n)),
                  pl.BlockSpec((1, 1, tn), lambda l, n: (l, 0, n))],
        out_specs=pl.BlockSpec((1, SUBLANES, tn), lambda l, n: (l, 0, n)),
        compiler_params=_cparams("parallel", "parallel"),
        name="ada_up",
    )(r, w_up, b_up.reshape(depth, 1, nmod))


def _modulate_kernel(x_ref, sh_ref, sc_ref, o_ref):
    o_ref[...] = (x_ref[...] * (1.0 + sc_ref[0]) + sh_ref[0]).astype(BF16)


def _modulate(x, sh, sc, seq_len):
    t, d = x.shape
    tm = min(512, seq_len)
    per = seq_len // tm
    vec = pl.BlockSpec((1, 1, d), lambda i: (i // per, 0, 0))
    return pl.pallas_call(
        _modulate_kernel,
        out_shape=jax.ShapeDtypeStruct((t, d), BF16),
        grid=(t // tm,),
        in_specs=[pl.BlockSpec((tm, d), lambda i: (i, 0)), vec, vec],
        out_specs=pl.BlockSpec((tm, d), lambda i: (i, 0)),
        compiler_params=_cparams("parallel"),
        name="modulate",
    )(x, sh, sc)


def _proj_kernel(a_ref, w_ref, o_ref, wb_ref):
    j = pl.program_id(0)

    @pl.when(pl.program_id(1) == 0)
    def _():
        wb_ref[...] = w_ref[...].astype(BF16)

    acc = _dot(a_ref[...], wb_ref[...])

    @pl.when(j == 0)
    def _():
        o_ref[...] = _gelu(acc).astype(BF16)

    @pl.when(j == 1)
    def _():
        g = _gelu(acc)
        dlt = g - jnp.mean(g, axis=-1, keepdims=True)
        var = jnp.mean(dlt * dlt, axis=-1, keepdims=True)
        o_ref[...] = (dlt * lax.rsqrt(var + LN_EPS)).astype(BF16)

    @pl.when(j >= 2)
    def _():
        o_ref[...] = acc.astype(BF16)


def _proj(h, w_in):
    t, d = h.shape
    n = w_in.shape[1]
    tm = min(512, t)
    return pl.pallas_call(
        _proj_kernel,
        out_shape=jax.ShapeDtypeStruct((t, n), BF16),
        grid=(n // BRANCH, t // tm),
        in_specs=[pl.BlockSpec((tm, d), lambda j, i: (i, 0)),
                  pl.BlockSpec((d, BRANCH), lambda j, i: (0, j))],
        out_specs=pl.BlockSpec((tm, BRANCH), lambda j, i: (i, j)),
        scratch_shapes=[pltpu.VMEM((d, BRANCH), BF16)],
        compiler_params=_cparams("parallel", "arbitrary"),
        name="proj",
    )(h, w_in)


def _s5_kernel(uf_ref, ub_ref, pf_ref, pb_ref, pft_ref, pbt_ref, bm_ref, cm_ref, lam_ref, h0_ref,
               yf_ref, yb_ref, hfin_ref, uf_s, ub_s, of_s, ob_s, bu_ref, st_ref, *, tblk, nseq):
    i = pl.program_id(0)
    half = S5_GB * S5_STATE
    src_rows = nseq * tblk

    @pl.when(i == 0)
    def _():
        st_ref[...] = h0_ref[...]

    uf_s[...] = _dot(pf_ref[...], uf_ref[...].reshape(src_rows, BRANCH)).astype(BF16)
    ub_s[...] = _dot(pb_ref[...], ub_ref[...].reshape(src_rows, BRANCH)).astype(BF16)

    for gb in range(S5_GB):
        cols = slice(gb * 128, (gb + 1) * 128)
        bu = bu_ref.at[gb % 2]
        bu[...] = _dot(jnp.concatenate([uf_s[:, cols], ub_s[:, cols]], axis=1), bm_ref[gb])
        lr = lam_ref[gb, 0]
        li = lam_ref[gb, 1]

        def step(t, carry, bu=bu, lr=lr, li=li):
            hr, hi = carry
            r0 = pl.multiple_of(t * SUBLANES, SUBLANES)
            nr = lr * hr - li * hi + bu[pl.ds(r0, SUBLANES), 0:half]
            ni = lr * hi + li * hr + bu[pl.ds(r0, SUBLANES), half:2 * half]
            bu[pl.ds(r0, SUBLANES), 0:half] = nr
            bu[pl.ds(r0, SUBLANES), half:2 * half] = ni
            return nr, ni

        hr, hi = lax.fori_loop(0, tblk, step, (st_ref[gb, 0], st_ref[gb, 1]), unroll=True)
        st_ref[gb, 0] = hr
        st_ref[gb, 1] = hi
        yy = _dot(bu[...].astype(BF16), cm_ref[gb]).astype(BF16)
        of_s[:, cols] = yy[:, :128]
        ob_s[:, cols] = yy[:, 128:]

    yf_ref[...] = _dot(pft_ref[...], of_s[...]).astype(BF16).reshape(nseq, tblk, BRANCH)
    yb_ref[...] = _dot(pbt_ref[...], ob_s[...]).astype(BF16).reshape(nseq, tblk, BRANCH)

    @pl.when(i == pl.num_programs(0) - 1)
    def _():
        hfin_ref[...] = st_ref[...]


def _s5_operators(a_re, a_im, log_step, b_re, b_im, c_re, c_im):
    a_re = a_re.astype(F32)
    a_im = a_im.astype(F32)
    step = jnp.exp(log_step.astype(F32))[..., None]
    mag = jnp.exp(a_re * step)
    lam_re = mag * jnp.cos(a_im * step)
    lam_im = mag * jnp.sin(a_im * step)
    den = a_re * a_re + a_im * a_im
    f_re = ((lam_re - 1.0) * a_re + lam_im * a_im) / den
    f_im = (lam_im * a_re - (lam_re - 1.0) * a_im) / den
    b_re = b_re.astype(F32)[None]
    b_im = b_im.astype(F32)[None]
    bb_re = f_re[..., None] * b_re - f_im[..., None] * b_im
    bb_im = f_re[..., None] * b_im + f_im[..., None] * b_re
    ngl = 128 // S5_GROUP
    eye = jnp.eye(ngl, dtype=F32)
    bb = jnp.stack([bb_re, bb_im], axis=1)
    bb = bb.reshape(2, 2, S5_GB, ngl, S5_STATE, S5_GROUP)
    bmat = jnp.einsum('drgaph,ab->gdahrbp', bb, eye).reshape(S5_GB, 2 * 128, 2 * ngl * S5_STATE)
    cc = jnp.stack([c_re.astype(F32), -c_im.astype(F32)], axis=1)
    cc = cc.reshape(2, 2, S5_GB, ngl, S5_GROUP, S5_STATE)
    cmat = jnp.einsum('drgahp,ab->grapdbh', cc, eye).reshape(S5_GB, 2 * ngl * S5_STATE, 2 * 128)
    lam = jnp.stack([lam_re, lam_im], axis=1)
    lam = lam.reshape(2, 2, S5_GB, ngl * S5_STATE).transpose(2, 1, 0, 3)
    lam = jnp.repeat(lam, SUBLANES // 2, axis=2)
    return bmat.astype(BF16), cmat.astype(BF16), lam


def _scan_order(nseq, tblk):
    step = jnp.arange(tblk * SUBLANES) // SUBLANES
    seq = jnp.arange(tblk * SUBLANES) % SUBLANES
    src = jnp.arange(nseq * tblk)[None, :]
    fwd = (seq < nseq)[:, None] & (src == (seq * tblk + step)[:, None])
    bwd = (seq >= nseq)[:, None] & (src == ((seq - nseq) * tblk + (tblk - 1 - step))[:, None])
    return fwd.astype(BF16), bwd.astype(BF16), fwd.T.astype(BF16), bwd.T.astype(BF16)


def _s5(proj, h0, ops, col_block):
    bs, ls, _ = proj.shape
    assert 2 * bs == SUBLANES
    bmat, cmat, lam = ops
    tblk = 64
    nblk = ls // tblk
    rows = tblk * SUBLANES
    perms = _scan_order(bs, tblk)
    st_shape = (S5_GB, 2, SUBLANES, S5_GB * S5_STATE)
    full = lambda shape: pl.BlockSpec(shape, lambda i: (0,) * len(shape))
    y_shape = jax.ShapeDtypeStruct((bs, ls, BRANCH), BF16)
    return pl.pallas_call(
        functools.partial(_s5_kernel, tblk=tblk, nseq=bs),
        out_shape=(y_shape, y_shape, jax.ShapeDtypeStruct(st_shape, F32)),
        grid=(nblk,),
        in_specs=[pl.BlockSpec((bs, tblk, BRANCH), lambda i: (0, i, col_block)),
                  pl.BlockSpec((bs, tblk, BRANCH), lambda i: (0, nblk - 1 - i, col_block))]
                 + [full(p.shape) for p in perms]
                 + [full(bmat.shape), full(cmat.shape), full(lam.shape), full(st_shape)],
        out_specs=(pl.BlockSpec((bs, tblk, BRANCH), lambda i: (0, i, 0)),
                   pl.BlockSpec((bs, tblk, BRANCH), lambda i: (0, nblk - 1 - i, 0)),
                   full(st_shape)),
        scratch_shapes=[pltpu.VMEM((rows, BRANCH), BF16)] * 4
                       + [pltpu.VMEM((2, rows, 2 * S5_GB * S5_STATE), F32), pltpu.VMEM(st_shape, F32)],
        compiler_params=_cparams("arbitrary"),
        name="s5_scan",
    )(proj, proj, *perms, bmat, cmat, lam, h0)


def _mixers_kernel(p_ref, yf_ref, yb_ref, ws_ref, bs_ref, sd_ref, wglu_ref, bglu_ref,
                   pm_ref, ic_ref, pw_ref, ps_ref, o_ref, *, tm):
    for ch in range(tm // CHUNK):
        r = slice(ch * CHUNK, (ch + 1) * CHUNK)
        for h in range(HEADS):
            vn = p_ref[r, BRANCH + h * 128:BRANCH + (h + 1) * 128]
            mixed = _dot(ws_ref[h], vn) + bs_ref[h]
            o_ref[r, h * 128:(h + 1) * 128] = (p_ref[r, h * 128:(h + 1) * 128].astype(F32) * mixed).astype(BF16)
    y = (yf_ref[...].astype(F32) + yb_ref[...].astype(F32)
         + p_ref[:, 2 * BRANCH:3 * BRANCH].astype(F32) * sd_ref[...])
    g = _gelu(y)
    z = _dot(g.astype(BF16), wglu_ref[...]) + bglu_ref[...]
    o_ref[:, BRANCH:2 * BRANCH] = (g * _sigmoid(z)).astype(BF16)
    for w in range(len(POOL_WINDOWS)):
        c0 = 3 * BRANCH + w * POOL_GROUP
        gq = p_ref[:, c0:c0 + POOL_GROUP]
        s = _dot(pm_ref[w], gq)
        yv = s * ic_ref[w] - gq.astype(F32)
        yc = _dot(yv.astype(BF16), pw_ref[w]) * ps_ref[:, w * POOL_GROUP:(w + 1) * POOL_GROUP]
        o0 = 2 * BRANCH + w * POOL_GROUP
        o_ref[:, o0:o0 + POOL_GROUP] = yc.astype(BF16)


def _pool_operators(tm, seg):
    t = jnp.arange(tm)
    base = (t // seg) * seg
    pos = t - base
    mats, invs = [], []
    for k in POOL_WINDOWS:
        lo = base + jnp.clip(pos - k // 2, 0, seg - 1)
        hi = base + jnp.clip(pos + k // 2 - 1, 0, seg - 1)
        mats.append(((t[None, :] >= lo[:, None]) & (t[None, :] <= hi[:, None])).astype(BF16))
        cnt = (hi - lo + 1).astype(F32)
        invs.append(jnp.broadcast_to((1.0 / cnt)[:, None], (tm, POOL_GROUP)))
    return jnp.stack(mats), jnp.stack(invs)


def _mixers(proj, yf, yb, prm, seg):
    t, n = proj.shape
    tm = 256
    assert tm % seg == 0 and tm % CHUNK == 0
    pm, ic = _pool_operators(tm, seg)
    full = lambda a: pl.BlockSpec(a.shape, lambda i: (0,) * a.ndim)
    ws, bs, sd, wglu, bglu, pw, ps = (prm[k] for k in ('ws', 'bs', 'sd', 'wglu', 'bglu', 'pw', 'ps'))
    return pl.pallas_call(
        functools.partial(_mixers_kernel, tm=tm),
        out_shape=jax.ShapeDtypeStruct((t, 3 * BRANCH), BF16),
        grid=(t // tm,),
        in_specs=[pl.BlockSpec((tm, n), lambda i: (i, 0)),
                  pl.BlockSpec((tm, BRANCH), lambda i: (i, 0)),
                  pl.BlockSpec((tm, BRANCH), lambda i: (i, 0)),
                  full(ws), full(bs), full(sd), full(wglu), full(bglu),
                  full(pm), full(ic), full(pw), full(ps)],
        out_specs=pl.BlockSpec((tm, 3 * BRANCH), lambda i: (i, 0)),
        compiler_params=_cparams("parallel"),
        name="mixers",
    )(proj, yf, yb, ws, bs, sd, wglu, bglu, pm, ic, pw, ps)


def _gate_merge_kernel(h_ref, y_ref, wg0_ref, wg1_ref, wg2_ref, bg_ref, wb_ref, o_ref, wgs_ref, wbs_ref):
    @pl.when(pl.program_id(1) == 0)
    def _():
        for k, wg_ref in enumerate((wg0_ref, wg1_ref, wg2_ref)):
            wgs_ref[k] = wg_ref[...].astype(BF16)
        wbs_ref[...] = wb_ref[...].astype(BF16)

    h = h_ref[...]
    acc = None
    for k in range(3):
        gate = _sigmoid(_dot(h, wgs_ref[k]) + bg_ref[k])
        term = gate * _dot(y_ref[:, k * BRANCH:(k + 1) * BRANCH], wbs_ref[k])
        acc = term if acc is None else acc + term
    o_ref[...] = acc.astype(BF16)


def _gate_merge(h, yabc, w_gate, bg, wb):
    t, d = h.shape
    tm, tn = min(512, t), 256
    nj = d // tn
    gate_cols = lambda k: pl.BlockSpec((d, tn), lambda j, i: (0, k * nj + j))
    return pl.pallas_call(
        _gate_merge_kernel,
        out_shape=jax.ShapeDtypeStruct((t, d), BF16),
        grid=(nj, t // tm),
        in_specs=[pl.BlockSpec((tm, d), lambda j, i: (i, 0)),
                  pl.BlockSpec((tm, 3 * BRANCH), lambda j, i: (i, 0)),
                  gate_cols(0), gate_cols(1), gate_cols(2),
                  pl.BlockSpec((3, 1, tn), lambda j, i: (0, 0, j)),
                  pl.BlockSpec((3, BRANCH, tn), lambda j, i: (0, 0, j))],
        out_specs=pl.BlockSpec((tm, tn), lambda j, i: (i, j)),
        scratch_shapes=[pltpu.VMEM((3, d, tn), BF16), pltpu.VMEM((3, BRANCH, tn), BF16)],
        compiler_params=_cparams("parallel", "arbitrary"),
        name="gate_merge",
    )(h, yabc, w_gate, w_gate, w_gate, bg, wb)


def _out_ln_kernel(m_ref, w_ref, x_ref, g_ref, lg_ref, lb_ref, sc_ref, sh_ref, wr_ref,
                   x1_ref, h2_ref, aff_ref, pre_ref, *, nj, tn, alpha, n_experts):
    j = pl.program_id(1)
    pre_ref[j] = alpha * x_ref[...] + g_ref[0] * _dot(m_ref[...], w_ref[...])

    @pl.when(j == nj - 1)
    def _():
        d = nj * tn
        tot = None
        for jj in range(nj):
            s = jnp.sum(pre_ref[jj], axis=-1, keepdims=True)
            tot = s if tot is None else tot + s
        mu = tot / d
        tot = None
        for jj in range(nj):
            dl = pre_ref[jj] - mu
            s = jnp.sum(dl * dl, axis=-1, keepdims=True)
            tot = s if tot is None else tot + s
        rstd = lax.rsqrt(tot / d + LN_EPS)
        for jj in range(nj):
            cs = slice(jj * tn, (jj + 1) * tn)
            y = (pre_ref[jj] - mu) * rstd * lg_ref[:, cs] + lb_ref[:, cs]
            x1_ref[:, cs] = y
            h2_ref[:, cs] = (y * (1.0 + sc_ref[0, :, cs]) + sh_ref[0, :, cs]).astype(BF16)
        logits = _dot(h2_ref[...], wr_ref[...])
        lane = lax.broadcasted_iota(jnp.int32, logits.shape, 1)
        logits = jnp.where(lane < n_experts, logits, -jnp.inf)
        e = jnp.exp(logits - jnp.max(logits, axis=-1, keepdims=True))
        aff_ref[...] = e / jnp.sum(e, axis=-1, keepdims=True)


def _out_ln(merged, w_out, x, gate, ln_g, ln_b, sc2, sh2, w_router, seq_len, alpha, n_experts):
    t, d = x.shape
    tm, tn = min(512, seq_len), 512
    per = seq_len // tm
    nj = d // tn
    row = pl.BlockSpec((1, d), lambda i, j: (0, 0))
    vec = pl.BlockSpec((1, 1, d), lambda i, j: (i // per, 0, 0))
    return pl.pallas_call(
        functools.partial(_out_ln_kernel, nj=nj, tn=tn, alpha=alpha, n_experts=n_experts),
        out_shape=(jax.ShapeDtypeStruct((t, d), F32),
                   jax.ShapeDtypeStruct((t, d), BF16),
                   jax.ShapeDtypeStruct((t, ROUTER_LANES), F32)),
        grid=(t // tm, nj),
        in_specs=[pl.BlockSpec((tm, d), lambda i, j: (i, 0)),
                  pl.BlockSpec((d, tn), lambda i, j: (0, j)),
                  pl.BlockSpec((tm, tn), lambda i, j: (i, j)),
                  pl.BlockSpec((1, 1, tn), lambda i, j: (i // per, 0, j)),
                  row, row, vec, vec,
                  pl.BlockSpec((d, ROUTER_LANES), lambda i, j: (0, 0))],
        out_specs=(pl.BlockSpec((tm, d), lambda i, j: (i, 0)),
                   pl.BlockSpec((tm, d), lambda i, j: (i, 0)),
                   pl.BlockSpec((tm, ROUTER_LANES), lambda i, j: (i, 0))),
        scratch_shapes=[pltpu.VMEM((nj, tm, tn), F32)],
        compiler_params=_cparams("parallel", "arbitrary"),
        name="out_ln",
    )(merged, w_out, x, gate, ln_g, ln_b, sc2, sh2, w_router)


def _ffn_kernel(x_ref, w1_ref, w3_ref, w2_ref, tw_ref, tag_ref, o_ref):
    d = x_ref.shape[2]
    x = x_ref[0]
    h1 = _dot(x, w1_ref[0])
    hid = (h1 * _sigmoid(h1)) * _dot(x, w3_ref[0])
    o_ref[0, :, :d] = (_dot(hid.astype(BF16), w2_ref[0]) * tw_ref[0]).astype(BF16)
    o_ref[0, :, d:] = tag_ref[0]


def _expert_ffn(xs, w1, w3, w2, tw, tag):
    e, m, d = xs.shape
    ff = w1.shape[2]
    tm = min(512, m)
    return pl.pallas_call(
        _ffn_kernel,
        out_shape=jax.ShapeDtypeStruct((e, m, d + TAG_LANES), BF16),
        grid=(e, m // tm),
        in_specs=[pl.BlockSpec((1, tm, d), lambda ei, i: (ei, i, 0)),
                  pl.BlockSpec((1, d, ff), lambda ei, i: (ei, 0, 0)),
                  pl.BlockSpec((1, d, ff), lambda ei, i: (ei, 0, 0)),
                  pl.BlockSpec((1, ff, d), lambda ei, i: (ei, 0, 0)),
                  pl.BlockSpec((1, tm, 1), lambda ei, i: (ei, i, 0)),
                  pl.BlockSpec((1, tm, TAG_LANES), lambda ei, i: (ei, i, 0))],
        out_specs=pl.BlockSpec((1, tm, d + TAG_LANES), lambda ei, i: (ei, i, 0)),
        compiler_params=_cparams("parallel", "parallel"),
        name="expert_ffn",
    )(xs, w1, w3, w2, tw, tag)


def _tag_tokens(tags):
    lane = lax.broadcasted_iota(jnp.int32, (SUBLANES, TAG_LANES), 1)
    coef = jnp.where(lane == 0, float(TAG_BASE), jnp.where(lane == 1, 1.0, 0.0)).astype(BF16)
    return lax.dot_general(coef, tags, (((1,), (1,)), ((), ())), preferred_element_type=F32)[0:1]


def _select_rows(tok, valid, first_token, tm):
    ids = (first_token + lax.broadcasted_iota(jnp.int32, (tm, tok.shape[1]), 0)).astype(F32)
    return jnp.where(jnp.where(valid, tok, -1.0) == ids, 1.0, 0.0).astype(BF16)


def _combine_ln2_kernel(s0_ref, cnt_ref, x_ref, oe_ref, g_ref, lg_ref, lb_ref, sc_ref, sh_ref,
                        x2_ref, h_ref, win_ref, xw_ref, acc_ref, sem, xsem,
                        *, alpha, n_exp, win, tm, total_rows):
    i = pl.program_id(0)
    d = x_ref.shape[1]
    slot = i % 2
    last_start = total_rows - win
    cover = win - BF16_ROW_TILE
    first_token = i * tm

    def window_start(first_row):
        start = (jnp.minimum(first_row, last_start) // BF16_ROW_TILE) * BF16_ROW_TILE
        return pl.multiple_of(start, BF16_ROW_TILE)

    def window_copy(start, buf, e):
        return pltpu.make_async_copy(oe_ref.at[pl.ds(start, win)],
                                     win_ref.at[buf, pl.ds(e * win, win)], sem.at[buf, e])

    def fetch(tile, buf):
        for e in range(n_exp):
            window_copy(window_start(s0_ref[tile * n_exp + e]), buf, e).start()

    @pl.when(i == 0)
    def _():
        fetch(0, 0)

    @pl.when(i + 1 < pl.num_programs(0))
    def _():
        fetch(i + 1, 1 - slot)

    for e in range(n_exp):
        window_copy(0, slot, e).wait()

    wl = lax.broadcasted_iota(jnp.int32, (1, n_exp * win), 1)
    row = jnp.zeros_like(wl)
    lo = jnp.zeros_like(wl)
    hi = jnp.zeros_like(wl)
    for e in range(n_exp):
        s0 = s0_ref[i * n_exp + e]
        cnt = cnt_ref[i * n_exp + e]
        in_e = (wl >= e * win) & (wl < (e + 1) * win)
        row = jnp.where(in_e, window_start(s0) + (wl - e * win), row)
        lo = jnp.where(in_e, s0, lo)
        hi = jnp.where(in_e, s0 + jnp.minimum(cnt, cover), hi)
    valid = (row >= lo) & (row < hi)
    tok = _tag_tokens(win_ref[slot, :, d:])
    acc_ref[...] = _dot(_select_rows(tok, valid, first_token, tm), win_ref[slot, :, :d])

    for e in range(n_exp):
        s0 = s0_ref[i * n_exp + e]
        cnt = cnt_ref[i * n_exp + e]

        def extra(k, carry, s0=s0, cnt=cnt):
            lo_k = s0 + k * cover
            start = window_start(lo_k)
            cp = pltpu.make_async_copy(oe_ref.at[pl.ds(start, win)], xw_ref, xsem.at[0])
            cp.start()
            cp.wait()
            row_k = start + lax.broadcasted_iota(jnp.int32, (1, win), 1)
            ok = (row_k >= lo_k) & (row_k < jnp.minimum(lo_k + cover, s0 + cnt))
            sel = _select_rows(_tag_tokens(xw_ref[:, d:]), ok, first_token, tm)
            acc_ref[...] += _dot(sel, xw_ref[:, :d])
            return carry

        lax.fori_loop(1, (cnt + cover - 1) // cover, extra, 0)

    pre = alpha * x_ref[...] + g_ref[0] * acc_ref[...]
    dl = pre - jnp.mean(pre, axis=-1, keepdims=True)
    var = jnp.mean(dl * dl, axis=-1, keepdims=True)
    y = dl * lax.rsqrt(var + LN_EPS) * lg_ref[...] + lb_ref[...]
    x2_ref[...] = y
    h_ref[...] = (y * (1.0 + sc_ref[0]) + sh_ref[0]).astype(BF16)


def _combine_ln2(x, oe, s0, cnt, gate, ln_g, ln_b, sc, sh, seq_len, tm, alpha, n_exp):
    t, d = x.shape
    total_rows = oe.shape[0]
    win = COMBINE_WINDOW
    assert total_rows >= win and total_rows % BF16_ROW_TILE == 0 and t <= TAG_BASE * 256
    per = seq_len // tm
    tile = pl.BlockSpec((tm, d), lambda i, *_: (i, 0))
    row = pl.BlockSpec((1, d), lambda i, *_: (0, 0))
    vec = pl.BlockSpec((1, 1, d), lambda i, *_: (i // per, 0, 0))
    return pl.pallas_call(
        functools.partial(_combine_ln2_kernel, alpha=alpha, n_exp=n_exp, win=win, tm=tm,
                          total_rows=total_rows),
        out_shape=(jax.ShapeDtypeStruct((t, d), F32), jax.ShapeDtypeStruct((t, d), BF16)),
        grid_spec=pltpu.PrefetchScalarGridSpec(
            num_scalar_prefetch=2,
            grid=(t // tm,),
            in_specs=[tile, pl.BlockSpec(memory_space=pl.ANY), vec, row, row, vec, vec],
            out_specs=(tile, tile),
            scratch_shapes=[pltpu.VMEM((2, n_exp * win, d + TAG_LANES), BF16),
                            pltpu.VMEM((win, d + TAG_LANES), BF16),
                            pltpu.VMEM((tm, d), F32),
                            pltpu.SemaphoreType.DMA((2, n_exp)),
                            pltpu.SemaphoreType.DMA((1,))]),
        compiler_params=_cparams("arbitrary"),
        name="combine_ln2",
    )(s0, cnt, x, oe, gate, ln_g, ln_b, sc, sh)


def _mixer_inputs(h, lp, bs, ls, h0):
    proj = _proj(h, lp['w_in'])
    yf, yb, hfin = _s5(proj.reshape(bs, ls, -1), h0, lp['s5_ops'], col_block=2)
    return proj, yf.reshape(bs * ls, BRANCH), yb.reshape(bs * ls, BRANCH), hfin


def _expert_choice(h2, aff, lp, bs, ls, tm):
    n_exp = lp['w1'].shape[0]
    d = h2.shape[1]
    cap = CAPACITY_FACTOR * ls // n_exp
    m = bs * cap
    a = aff[:, :n_exp].reshape(bs, ls, n_exp)
    top_w, top_idx = lax.top_k(jnp.swapaxes(a, 1, 2), cap)
    top_idx, top_w = lax.sort((top_idx, top_w), dimension=2, num_keys=1)
    gidx = top_idx + (jnp.arange(bs, dtype=top_idx.dtype) * ls)[:, None, None]
    gidx = gidx.transpose(1, 0, 2).reshape(-1)
    xs = h2.at[gidx].get(mode='promise_in_bounds').reshape(n_exp, m, d)
    tw = top_w.transpose(1, 0, 2).reshape(n_exp, m, 1)
    tag = jnp.stack([gidx // TAG_BASE, gidx % TAG_BASE], axis=-1).astype(BF16)
    tag = jnp.pad(tag, ((0, 0), (0, TAG_LANES - 2))).reshape(n_exp, m, TAG_LANES)
    oe = _expert_ffn(xs, lp['w1'], lp['w3'], lp['w2'], tw, tag)
    tile_start = jnp.arange(ls // tm, dtype=top_idx.dtype) * tm
    below = jnp.sum(top_idx[:, :, None, :] < tile_start[None, None, :, None], axis=-1)
    upto = jnp.concatenate([below[:, :, 1:], jnp.full((bs, n_exp, 1), cap, below.dtype)], axis=2)
    base = (jnp.arange(n_exp) * m)[None, :, None] + (jnp.arange(bs) * cap)[:, None, None]
    s0 = (base + below).transpose(0, 2, 1).reshape(-1).astype(jnp.int32)
    cnt = (upto - below).transpose(0, 2, 1).reshape(-1).astype(jnp.int32)
    return oe.reshape(n_exp * m, d + TAG_LANES), s0, cnt


def _layer(x, h, mod, nxt, lp, bs, ls, seg, h0, alpha):
    proj, yf, yb, hfin = _mixer_inputs(h, lp, bs, ls, h0)
    yabc = _mixers(proj, yf, yb, lp, seg)
    merged = _gate_merge(h, yabc, lp['wg'], lp['bg'], lp['wb'])
    n_exp = lp['w1'].shape[0]
    x1, h2, aff = _out_ln(merged, lp['w_out'], x, mod[2], lp['ln1_g'], lp['ln1_b'], mod[4], mod[3],
                          lp['w_router'], ls, alpha, n_exp)
    tm = min(256, ls)
    oe, s0, cnt = _expert_choice(h2, aff, lp, bs, ls, tm)
    x2, hn = _combine_ln2(x1, oe, s0, cnt, mod[5], lp['ln2_g'], lp['ln2_b'], nxt[1], nxt[0],
                          ls, tm, alpha, n_exp)
    return x2, hn, hfin


def kernel(x, c, ctx, c_ctx, ada_w_down, ada_w_up, ada_b, w_in, gmlp_w_s, gmlp_b_s, s5_a_re, s5_a_im, s5_log_step, s5_b_re, s5_b_im, s5_c_re, s5_c_im, s5_d, s5_w_glu, s5_b_glu, pool_w, pool_scale, w_gate, b_gate, w_branch, w_out, ln1_g, ln1_b, w_router, w1, w3, w2, ln2_g, ln2_b):
    bs, n, d = x.shape
    lc = ctx.shape[1]
    depth = w_in.shape[0]
    n_exp = w_router.shape[2]
    alpha = (2.0 * depth) ** 0.25
    assert bs + 1 <= SUBLANES and n % GRID_W == 0

    cond = jnp.zeros((SUBLANES, d), F32).at[:bs].set(c).at[bs].set(c_ctx)
    mods = _ada(cond, ada_w_down, ada_w_up, ada_b).reshape(depth, SUBLANES, N_MOD, d)

    def lat_mod(l):
        return mods[l, :bs].transpose(1, 0, 2)[:, :, None, :]

    def ctx_mod(l):
        return jnp.broadcast_to(mods[l, bs][:, None, None, :], (N_MOD, bs, 1, d))

    def layer_params(l):
        return dict(
            w_in=w_in[l],
            s5_ops=_s5_operators(s5_a_re[l], s5_a_im[l], s5_log_step[l], s5_b_re[l], s5_b_im[l],
                                 s5_c_re[l], s5_c_im[l]),
            ws=gmlp_w_s[l].astype(BF16),
            bs=jnp.broadcast_to(gmlp_b_s[l][:, :, None], (HEADS, CHUNK, 128)).astype(F32),
            sd=s5_d[l].reshape(1, BRANCH), wglu=s5_w_glu[l].astype(BF16), bglu=s5_b_glu[l].reshape(1, BRANCH),
            pw=pool_w[l].astype(BF16), ps=pool_scale[l].reshape(1, BRANCH),
            wg=w_gate[l], bg=b_gate[l].reshape(3, 1, d), wb=w_branch[l],
            w_out=w_out[l].astype(BF16), ln1_g=ln1_g[l].reshape(1, d), ln1_b=ln1_b[l].reshape(1, d),
            w_router=jnp.zeros((d, ROUTER_LANES), BF16).at[:, :n_exp].set(w_router[l].astype(BF16)),
            w1=w1[l].astype(BF16), w3=w3[l].astype(BF16), w2=w2[l].astype(BF16),
            ln2_g=ln2_g[l].reshape(1, d), ln2_b=ln2_b[l].reshape(1, d))

    xl = x.reshape(bs * n, d)
    xc = ctx.reshape(bs * lc, d)
    m0, c0 = lat_mod(0), ctx_mod(0)
    hl = _modulate(xl, m0[0], m0[1], n)
    hc = _modulate(xc, c0[0], c0[1], lc)
    zero_state = jnp.zeros((S5_GB, 2, SUBLANES, S5_GB * S5_STATE), F32)
    for l in range(depth):
        lp = layer_params(l)
        nl = min(l + 1, depth - 1)
        if l < depth - 1:
            xc, hc, ctx_final = _layer(xc, hc, ctx_mod(l), ctx_mod(nl), lp, bs, lc, lc, zero_state, alpha)
        else:
            ctx_final = _mixer_inputs(hc, lp, bs, lc, zero_state)[-1]
        xl, hl, _ = _layer(xl, hl, lat_mod(l), lat_mod(nl), lp, bs, n, GRID_W, ctx_final, alpha)
    return xl.reshape(bs, n, d)
```

```python
import functools
import math

import jax
import jax.numpy as jnp
from jax import lax
from jax.experimental import pallas as pl
from jax.experimental.pallas import tpu as pltpu

F32 = jnp.float32
BF16 = jnp.bfloat16

LN_EPS = 1e-6
GRID_W = 64
CHUNK = 128
BRANCH = 1024
HEADS = 8
S5_GROUP = 16
S5_STATE = 64
S5_GB = 8
POOL_WINDOWS = (2, 4, 8, 16)
POOL_GROUP = BRANCH // len(POOL_WINDOWS)
N_MOD = 6
ROUTER_LANES = 128
CAPACITY_FACTOR = 2
TAG_LANES = 128
TAG_BASE = 128
COMBINE_WINDOW = 64

V7X_VMEM_LIMIT = 60 * 1024 * 1024
SUBLANES = 8
BF16_ROW_TILE = 16


def _cparams(*sem):
    return pltpu.CompilerParams(dimension_semantics=sem, vmem_limit_bytes=V7X_VMEM_LIMIT)


def _sigmoid(x):
    return 1.0 / (1.0 + jnp.exp(-x))


def _gelu(x):
    c = math.sqrt(2.0 / math.pi)
    return x * (0.5 * (1.0 + jnp.tanh(c * (x + 0.044715 * (x * x * x)))))


def _dot(a, b):
    return jnp.dot(a, b, preferred_element_type=F32)


def _ada_down_kernel(c_ref, w_ref, o_ref):
    @pl.when(pl.program_id(1) == 0)
    def _():
        o_ref[...] = jnp.zeros_like(o_ref)

    c = c_ref[...]
    o_ref[0] += _dot((c * _sigmoid(c)).astype(BF16), w_ref[0].astype(BF16))


def _ada_up_kernel(r_ref, w_ref, b_ref, o_ref):
    o_ref[0] = _dot(r_ref[0].astype(BF16), w_ref[0].astype(BF16)) + b_ref[0]


def _ada(cond, w_down, w_up, b_up):
    depth, d, rank = w_down.shape
    nmod = w_up.shape[2]
    tk, tn = 1024, 2048
    r = pl.pallas_call(
        _ada_down_kernel,
        out_shape=jax.ShapeDtypeStruct((depth, SUBLANES, rank), F32),
        grid=(depth, d // tk),
        in_specs=[pl.BlockSpec((SUBLANES, tk), lambda l, k: (0, k)),
                  pl.BlockSpec((1, tk, rank), lambda l, k: (l, k, 0))],
        out_specs=pl.BlockSpec((1, SUBLANES, rank), lambda l, k: (l, 0, 0)),
        compiler_params=_cparams("parallel", "arbitrary"),
        name="ada_down",
    )(cond, w_down)
    return pl.pallas_call(
        _ada_up_kernel,
        out_shape=jax.ShapeDtypeStruct((depth, SUBLANES, nmod), F32),
        grid=(depth, nmod // tn),
        in_specs=[pl.BlockSpec((1, SUBLANES, rank), lambda l, n: (l, 0, 0)),
                  pl.BlockSpec((1, rank, tn), lambda l, n: (l, 0, n)),
                  pl.BlockSpec((1, 1, tn), lambda l, n: (l, 0, n))],
        out_specs=pl.BlockSpec((1, SUBLANES, tn), lambda l, n: (l, 0, n)),
        compiler_params=_cparams("parallel", "parallel"),
        name="ada_up",
    )(r, w_up, b_up.reshape(depth, 1, nmod))


def _modulate_kernel(x_ref, sh_ref, sc_ref, o_ref):
    o_ref[...] = (x_ref[...] * (1.0 + sc_ref[0]) + sh_ref[0]).astype(BF16)


def _modulate(x, sh, sc, seq_len):
    t, d = x.shape
    tm = min(512, seq_len)
    per = seq_len // tm
    vec = pl.BlockSpec((1, 1, d), lambda i: (i // per, 0, 0))
    return pl.pallas_call(
        _modulate_kernel,
        out_shape=jax.ShapeDtypeStruct((t, d), BF16),
        grid=(t // tm,),
        in_specs=[pl.BlockSpec((tm, d), lambda i: (i, 0)), vec, vec],
        out_specs=pl.BlockSpec((tm, d), lambda i: (i, 0)),
        compiler_params=_cparams("parallel"),
        name="modulate",
    )(x, sh, sc)


def _proj_kernel(a_ref, w_ref, o_ref, wb_ref):
    j = pl.program_id(0)

    @pl.when(pl.program_id(1) == 0)
    def _():
        wb_ref[...] = w_ref[...].astype(BF16)

    acc = _dot(a_ref[...], wb_ref[...])

    @pl.when(j == 0)
    def _():
        o_ref[...] = _gelu(acc).astype(BF16)

    @pl.when(j == 1)
    def _():
        g = _gelu(acc)
        dlt = g - jnp.mean(g, axis=-1, keepdims=True)
        var = jnp.mean(dlt * dlt, axis=-1, keepdims=True)
        o_ref[...] = (dlt * lax.rsqrt(var + LN_EPS)).astype(BF16)

    @pl.when(j >= 2)
    def _():
        o_ref[...] = acc.astype(BF16)


def _proj(h, w_in, layer):
    t, d = h.shape
    n = w_in.shape[2]
    tm = min(512, t)
    return pl.pallas_call(
        _proj_kernel,
        out_shape=jax.ShapeDtypeStruct((t, n), BF16),
        grid=(n // BRANCH, t // tm),
        in_specs=[pl.BlockSpec((tm, d), lambda j, i: (i, 0)),
                  pl.BlockSpec((None, d, BRANCH), lambda j, i: (layer, 0, j))],
        out_specs=pl.BlockSpec((tm, BRANCH), lambda j, i: (i, j)),
        scratch_shapes=[pltpu.VMEM((d, BRANCH), BF16)],
        compiler_params=_cparams("parallel", "arbitrary"),
        name="proj",
    )(h, w_in)


def _s5_kernel(uf_ref, ub_ref, pf_ref, pb_ref, pft_ref, pbt_ref, bm_ref, cm_ref, lam_ref, h0_ref,
               yf_ref, yb_ref, hfin_ref, uf_s, ub_s, of_s, ob_s, bu_ref, st_ref, *, tblk, nseq):
    i = pl.program_id(0)
    half = S5_GB * S5_STATE
    src_rows = nseq * tblk

    @pl.when(i == 0)
    def _():
        st_ref[...] = h0_ref[...]

    uf_s[...] = _dot(pf_ref[...], uf_ref[...].reshape(src_rows, BRANCH)).astype(BF16)
    ub_s[...] = _dot(pb_ref[...], ub_ref[...].reshape(src_rows, BRANCH)).astype(BF16)

    for gb in range(S5_GB):
        cols = slice(gb * 128, (gb + 1) * 128)
        bu = bu_ref.at[gb % 2]
        bu[...] = _dot(jnp.concatenate([uf_s[:, cols], ub_s[:, cols]], axis=1), bm_ref[gb])
        lr = lam_ref[gb, 0]
        li = lam_ref[gb, 1]

        def step(t, carry, bu=bu, lr=lr, li=li):
            hr, hi = carry
            r0 = pl.multiple_of(t * SUBLANES, SUBLANES)
            nr = lr * hr - li * hi + bu[pl.ds(r0, SUBLANES), 0:half]
            ni = lr * hi + li * hr + bu[pl.ds(r0, SUBLANES), half:2 * half]
            bu[pl.ds(r0, SUBLANES), 0:half] = nr
            bu[pl.ds(r0, SUBLANES), half:2 * half] = ni
            return nr, ni

        hr, hi = lax.fori_loop(0, tblk, step, (st_ref[gb, 0], st_ref[gb, 1]), unroll=True)
        st_ref[gb, 0] = hr
        st_ref[gb, 1] = hi
        yy = _dot(bu[...].astype(BF16), cm_ref[gb]).astype(BF16)
        of_s[:, cols] = yy[:, :128]
        ob_s[:, cols] = yy[:, 128:]

    yf_ref[...] = _dot(pft_ref[...], of_s[...]).astype(BF16).reshape(nseq, tblk, BRANCH)
    yb_ref[...] = _dot(pbt_ref[...], ob_s[...]).astype(BF16).reshape(nseq, tblk, BRANCH)

    @pl.when(i == pl.num_programs(0) - 1)
    def _():
        hfin_ref[...] = st_ref[...]


def _s5_operators(a_re, a_im, log_step, b_re, b_im, c_re, c_im):
    a_re = a_re.astype(F32)
    a_im = a_im.astype(F32)
    step = jnp.exp(log_step.astype(F32))[..., None]
    mag = jnp.exp(a_re * step)
    lam_re = mag * jnp.cos(a_im * step)
    lam_im = mag * jnp.sin(a_im * step)
    den = a_re * a_re + a_im * a_im
    f_re = ((lam_re - 1.0) * a_re + lam_im * a_im) / den
    f_im = (lam_im * a_re - (lam_re - 1.0) * a_im) / den
    b_re = b_re.astype(F32)[None]
    b_im = b_im.astype(F32)[None]
    bb_re = f_re[..., None] * b_re - f_im[..., None] * b_im
    bb_im = f_re[..., None] * b_im + f_im[..., None] * b_re
    ngl = 128 // S5_GROUP
    eye = jnp.eye(ngl, dtype=F32)
    bb = jnp.stack([bb_re, bb_im], axis=1)
    bb = bb.reshape(2, 2, S5_GB, ngl, S5_STATE, S5_GROUP)
    bmat = jnp.einsum('drgaph,ab->gdahrbp', bb, eye).reshape(S5_GB, 2 * 128, 2 * ngl * S5_STATE)
    cc = jnp.stack([c_re.astype(F32), -c_im.astype(F32)], axis=1)
    cc = cc.reshape(2, 2, S5_GB, ngl, S5_GROUP, S5_STATE)
    cmat = jnp.einsum('drgahp,ab->grapdbh', cc, eye).reshape(S5_GB, 2 * ngl * S5_STATE, 2 * 128)
    lam = jnp.stack([lam_re, lam_im], axis=1)
    lam = lam.reshape(2, 2, S5_GB, ngl * S5_STATE).transpose(2, 1, 0, 3)
    lam = jnp.repeat(lam, SUBLANES // 2, axis=2)
    return bmat.astype(BF16), cmat.astype(BF16), lam


def _scan_order(nseq, tblk):
    step = jnp.arange(tblk * SUBLANES) // SUBLANES
    seq = jnp.arange(tblk * SUBLANES) % SUBLANES
    src = jnp.arange(nseq * tblk)[None, :]
    fwd = (seq < nseq)[:, None] & (src == (seq * tblk + step)[:, None])
    bwd = (seq >= nseq)[:, None] & (src == ((seq - nseq) * tblk + (tblk - 1 - step))[:, None])
    return fwd.astype(BF16), bwd.astype(BF16), fwd.T.astype(BF16), bwd.T.astype(BF16)


def _s5(proj, h0, ops, col_block):
    bs, ls, _ = proj.shape
    assert 2 * bs == SUBLANES
    bmat, cmat, lam = ops
    tblk = 64
    nblk = ls // tblk
    rows = tblk * SUBLANES
    perms = _scan_order(bs, tblk)
    st_shape = (S5_GB, 2, SUBLANES, S5_GB * S5_STATE)
    full = lambda shape: pl.BlockSpec(shape, lambda i: (0,) * len(shape))
    y_shape = jax.ShapeDtypeStruct((bs, ls, BRANCH), BF16)
    return pl.pallas_call(
        functools.partial(_s5_kernel, tblk=tblk, nseq=bs),
        out_shape=(y_shape, y_shape, jax.ShapeDtypeStruct(st_shape, F32)),
        grid=(nblk,),
        in_specs=[pl.BlockSpec((bs, tblk, BRANCH), lambda i: (0, i, col_block)),
                  pl.BlockSpec((bs, tblk, BRANCH), lambda i: (0, nblk - 1 - i, col_block))]
                 + [full(p.shape) for p in perms]
                 + [full(bmat.shape), full(cmat.shape), full(lam.shape), full(st_shape)],
        out_specs=(pl.BlockSpec((bs, tblk, BRANCH), lambda i: (0, i, 0)),
                   pl.BlockSpec((bs, tblk, BRANCH), lambda i: (0, nblk - 1 - i, 0)),
                   full(st_shape)),
        scratch_shapes=[pltpu.VMEM((rows, BRANCH), BF16)] * 4
                       + [pltpu.VMEM((2, rows, 2 * S5_GB * S5_STATE), F32), pltpu.VMEM(st_shape, F32)],
        compiler_params=_cparams("arbitrary"),
        name="s5_scan",
    )(proj, proj, *perms, bmat, cmat, lam, h0)


def _mixers_kernel(p_ref, yf_ref, yb_ref, ws_ref, bs_ref, sd_ref, wglu_ref, bglu_ref,
                   pm_ref, ic_ref, pw_ref, ps_ref, o_ref, *, tm):
    for ch in range(tm // CHUNK):
        r = slice(ch * CHUNK, (ch + 1) * CHUNK)
        for h in range(HEADS):
            vn = p_ref[r, BRANCH + h * 128:BRANCH + (h + 1) * 128]
            mixed = _dot(ws_ref[h], vn) + bs_ref[h]
            o_ref[r, h * 128:(h + 1) * 128] = (p_ref[r, h * 128:(h + 1) * 128].astype(F32) * mixed).astype(BF16)
    y = (yf_ref[...].astype(F32) + yb_ref[...].astype(F32)
         + p_ref[:, 2 * BRANCH:3 * BRANCH].astype(F32) * sd_ref[...])
    g = _gelu(y)
    z = _dot(g.astype(BF16), wglu_ref[...]) + bglu_ref[...]
    o_ref[:, BRANCH:2 * BRANCH] = (g * _sigmoid(z)).astype(BF16)
    for w in range(len(POOL_WINDOWS)):
        c0 = 3 * BRANCH + w * POOL_GROUP
        gq = p_ref[:, c0:c0 + POOL_GROUP]
        s = _dot(pm_ref[w], gq)
        yv = s * ic_ref[w] - gq.astype(F32)
        yc = _dot(yv.astype(BF16), pw_ref[w]) * ps_ref[:, w * POOL_GROUP:(w + 1) * POOL_GROUP]
        o0 = 2 * BRANCH + w * POOL_GROUP
        o_ref[:, o0:o0 + POOL_GROUP] = yc.astype(BF16)


def _pool_operators(tm, seg):
    t = jnp.arange(tm)
    base = (t // seg) * seg
    pos = t - base
    mats, invs = [], []
    for k in POOL_WINDOWS:
        lo = base + jnp.clip(pos - k // 2, 0, seg - 1)
        hi = base + jnp.clip(pos + k // 2 - 1, 0, seg - 1)
        mats.append(((t[None, :] >= lo[:, None]) & (t[None, :] <= hi[:, None])).astype(BF16))
        cnt = (hi - lo + 1).astype(F32)
        invs.append(jnp.broadcast_to((1.0 / cnt)[:, None], (tm, POOL_GROUP)))
    return jnp.stack(mats), jnp.stack(invs)


def _mixers(proj, yf, yb, prm, seg):
    t, n = proj.shape
    tm = 256
    assert tm % seg == 0 and tm % CHUNK == 0
    pm, ic = _pool_operators(tm, seg)
    full = lambda a: pl.BlockSpec(a.shape, lambda i: (0,) * a.ndim)
    ws, bs, sd, wglu, bglu, pw, ps = (prm[k] for k in ('ws', 'bs', 'sd', 'wglu', 'bglu', 'pw', 'ps'))
    return pl.pallas_call(
        functools.partial(_mixers_kernel, tm=tm),
        out_shape=jax.ShapeDtypeStruct((t, 3 * BRANCH), BF16),
        grid=(t // tm,),
        in_specs=[pl.BlockSpec((tm, n), lambda i: (i, 0)),
                  pl.BlockSpec((tm, BRANCH), lambda i: (i, 0)),
                  pl.BlockSpec((tm, BRANCH), lambda i: (i, 0)),
                  full(ws), full(bs), full(sd), full(wglu), full(bglu),
                  full(pm), full(ic), full(pw), full(ps)],
        out_specs=pl.BlockSpec((tm, 3 * BRANCH), lambda i: (i, 0)),
        compiler_params=_cparams("parallel"),
        name="mixers",
    )(proj, yf, yb, ws, bs, sd, wglu, bglu, pm, ic, pw, ps)


def _gate_merge_kernel(h_ref, y_ref, wg0_ref, wg1_ref, wg2_ref, bg_ref, wb_ref, o_ref):
    h = h_ref[...]
    acc = None
    for k, wg_ref in enumerate((wg0_ref, wg1_ref, wg2_ref)):
        gate = _sigmoid(_dot(h, wg_ref[...]) + bg_ref[k])
        term = gate * _dot(y_ref[:, k * BRANCH:(k + 1) * BRANCH], wb_ref[k])
        acc = term if acc is None else acc + term
    o_ref[...] = acc.astype(BF16)


def _gate_merge(h, yabc, w_gate, bg, wb, layer):
    t, d = h.shape
    tm, tn = min(512, t), 512
    nj = d // tn
    gate_cols = lambda k: pl.BlockSpec((None, d, tn), lambda j, i: (layer, 0, k * nj + j))
    return pl.pallas_call(
        _gate_merge_kernel,
        out_shape=jax.ShapeDtypeStruct((t, d), BF16),
        grid=(nj, t // tm),
        in_specs=[pl.BlockSpec((tm, d), lambda j, i: (i, 0)),
                  pl.BlockSpec((tm, 3 * BRANCH), lambda j, i: (i, 0)),
                  gate_cols(0), gate_cols(1), gate_cols(2),
                  pl.BlockSpec((3, 1, tn), lambda j, i: (0, 0, j)),
                  pl.BlockSpec((None, 3, BRANCH, tn), lambda j, i: (layer, 0, 0, j))],
        out_specs=pl.BlockSpec((tm, tn), lambda j, i: (i, j)),
        compiler_params=_cparams("parallel", "parallel"),
        name="gate_merge",
    )(h, yabc, w_gate, w_gate, w_gate, bg, wb)


def _out_ln_kernel(m_ref, w_ref, x_ref, g_ref, lg_ref, lb_ref, sc_ref, sh_ref, wr_ref,
                   x1_ref, h2_ref, aff_ref, pre_ref, *, nj, tn, alpha, n_experts):
    j = pl.program_id(1)
    pre_ref[j] = alpha * x_ref[...] + g_ref[0] * _dot(m_ref[...], w_ref[...])

    @pl.when(j == nj - 1)
    def _():
        d = nj * tn
        tot = None
        for jj in range(nj):
            s = jnp.sum(pre_ref[jj], axis=-1, keepdims=True)
            tot = s if tot is None else tot + s
        mu = tot / d
        tot = None
        for jj in range(nj):
            dl = pre_ref[jj] - mu
            s = jnp.sum(dl * dl, axis=-1, keepdims=True)
            tot = s if tot is None else tot + s
        rstd = lax.rsqrt(tot / d + LN_EPS)
        for jj in range(nj):
            cs = slice(jj * tn, (jj + 1) * tn)
            y = (pre_ref[jj] - mu) * rstd * lg_ref[:, cs] + lb_ref[:, cs]
            x1_ref[:, cs] = y
            h2_ref[:, cs] = (y * (1.0 + sc_ref[0, :, cs]) + sh_ref[0, :, cs]).astype(BF16)
        logits = _dot(h2_ref[...], wr_ref[...])
        lane = lax.broadcasted_iota(jnp.int32, logits.shape, 1)
        logits = jnp.where(lane < n_experts, logits, -jnp.inf)
        e = jnp.exp(logits - jnp.max(logits, axis=-1, keepdims=True))
        aff_ref[...] = e / jnp.sum(e, axis=-1, keepdims=True)


def _out_ln(merged, w_out, x, gate, ln_g, ln_b, sc2, sh2, w_router, layer, seq_len, alpha, n_experts):
    t, d = x.shape
    tm, tn = min(512, seq_len), 512
    per = seq_len // tm
    nj = d // tn
    row = pl.BlockSpec((1, d), lambda i, j: (0, 0))
    vec = pl.BlockSpec((1, 1, d), lambda i, j: (i // per, 0, 0))
    return pl.pallas_call(
        functools.partial(_out_ln_kernel, nj=nj, tn=tn, alpha=alpha, n_experts=n_experts),
        out_shape=(jax.ShapeDtypeStruct((t, d), F32),
                   jax.ShapeDtypeStruct((t, d), BF16),
                   jax.ShapeDtypeStruct((t, ROUTER_LANES), F32)),
        grid=(t // tm, nj),
        in_specs=[pl.BlockSpec((tm, d), lambda i, j: (i, 0)),
                  pl.BlockSpec((None, d, tn), lambda i, j: (layer, 0, j)),
                  pl.BlockSpec((tm, tn), lambda i, j: (i, j)),
                  pl.BlockSpec((1, 1, tn), lambda i, j: (i // per, 0, j)),
                  row, row, vec, vec,
                  pl.BlockSpec((d, ROUTER_LANES), lambda i, j: (0, 0))],
        out_specs=(pl.BlockSpec((tm, d), lambda i, j: (i, 0)),
                   pl.BlockSpec((tm, d), lambda i, j: (i, 0)),
                   pl.BlockSpec((tm, ROUTER_LANES), lambda i, j: (i, 0))),
        scratch_shapes=[pltpu.VMEM((nj, tm, tn), F32)],
        compiler_params=_cparams("parallel", "arbitrary"),
        name="out_ln",
    )(merged, w_out, x, gate, ln_g, ln_b, sc2, sh2, w_router)


def _ffn_kernel(x_ref, w1_ref, w3_ref, w2_ref, tw_ref, tag_ref, o_ref):
    d = x_ref.shape[2]
    x = x_ref[0]
    h1 = _dot(x, w1_ref[0])
    hid = (h1 * _sigmoid(h1)) * _dot(x, w3_ref[0])
    o_ref[0, :, :d] = (_dot(hid.astype(BF16), w2_ref[0]) * tw_ref[0]).astype(BF16)
    o_ref[0, :, d:] = tag_ref[0]


def _expert_ffn(xs, w1, w3, w2, tw, tag, layer):
    e, m, d = xs.shape
    ff = w1.shape[3]
    tm = min(512, m)
    return pl.pallas_call(
        _ffn_kernel,
        out_shape=jax.ShapeDtypeStruct((e, m, d + TAG_LANES), BF16),
        grid=(e, m // tm),
        in_specs=[pl.BlockSpec((1, tm, d), lambda ei, i: (ei, i, 0)),
                  pl.BlockSpec((None, 1, d, ff), lambda ei, i: (layer, ei, 0, 0)),
                  pl.BlockSpec((None, 1, d, ff), lambda ei, i: (layer, ei, 0, 0)),
                  pl.BlockSpec((None, 1, ff, d), lambda ei, i: (layer, ei, 0, 0)),
                  pl.BlockSpec((1, tm, 1), lambda ei, i: (ei, i, 0)),
                  pl.BlockSpec((1, tm, TAG_LANES), lambda ei, i: (ei, i, 0))],
        out_specs=pl.BlockSpec((1, tm, d + TAG_LANES), lambda ei, i: (ei, i, 0)),
        compiler_params=_cparams("parallel", "parallel"),
        name="expert_ffn",
    )(xs, w1, w3, w2, tw, tag)


def _tag_tokens(tags):
    lane = lax.broadcasted_iota(jnp.int32, (SUBLANES, TAG_LANES), 1)
    coef = jnp.where(lane == 0, float(TAG_BASE), jnp.where(lane == 1, 1.0, 0.0)).astype(BF16)
    return lax.dot_general(coef, tags, (((1,), (1,)), ((), ())), preferred_element_type=F32)[0:1]


def _select_rows(tok, valid, first_token, tm):
    ids = (first_token + lax.broadcasted_iota(jnp.int32, (tm, tok.shape[1]), 0)).astype(F32)
    return jnp.where(jnp.where(valid, tok, -1.0) == ids, 1.0, 0.0).astype(BF16)


def _combine_ln2_kernel(s0_ref, cnt_ref, x_ref, oe_ref, g_ref, lg_ref, lb_ref, sc_ref, sh_ref,
                        x2_ref, h_ref, win_ref, xw_ref, acc_ref, sem, xsem,
                        *, alpha, n_exp, win, tm, total_rows):
    i = pl.program_id(0)
    d = x_ref.shape[1]
    slot = i % 2
    last_start = total_rows - win
    cover = win - BF16_ROW_TILE
    first_token = i * tm

    def window_start(first_row):
        start = (jnp.minimum(first_row, last_start) // BF16_ROW_TILE) * BF16_ROW_TILE
        return pl.multiple_of(start, BF16_ROW_TILE)

    def window_copy(start, buf, e):
        return pltpu.make_async_copy(oe_ref.at[pl.ds(start, win)],
                                     win_ref.at[buf, pl.ds(e * win, win)], sem.at[buf, e])

    def fetch(tile, buf):
        for e in range(n_exp):
            window_copy(window_start(s0_ref[tile * n_exp + e]), buf, e).start()

    @pl.when(i == 0)
    def _():
        fetch(0, 0)

    @pl.when(i + 1 < pl.num_programs(0))
    def _():
        fetch(i + 1, 1 - slot)

    for e in range(n_exp):
        window_copy(0, slot, e).wait()

    wl = lax.broadcasted_iota(jnp.int32, (1, n_exp * win), 1)
    row = jnp.zeros_like(wl)
    lo = jnp.zeros_like(wl)
    hi = jnp.zeros_like(wl)
    for e in range(n_exp):
        s0 = s0_ref[i * n_exp + e]
        cnt = cnt_ref[i * n_exp + e]
        in_e = (wl >= e * win) & (wl < (e + 1) * win)
        row = jnp.where(in_e, window_start(s0) + (wl - e * win), row)
        lo = jnp.where(in_e, s0, lo)
        hi = jnp.where(in_e, s0 + jnp.minimum(cnt, cover), hi)
    valid = (row >= lo) & (row < hi)
    tok = _tag_tokens(win_ref[slot, :, d:])
    acc_ref[...] = _dot(_select_rows(tok, valid, first_token, tm), win_ref[slot, :, :d])

    for e in range(n_exp):
        s0 = s0_ref[i * n_exp + e]
        cnt = cnt_ref[i * n_exp + e]

        def extra(k, carry, s0=s0, cnt=cnt):
            lo_k = s0 + k * cover
            start = window_start(lo_k)
            cp = pltpu.make_async_copy(oe_ref.at[pl.ds(start, win)], xw_ref, xsem.at[0])
            cp.start()
            cp.wait()
            row_k = start + lax.broadcasted_iota(jnp.int32, (1, win), 1)
            ok = (row_k >= lo_k) & (row_k < jnp.minimum(lo_k + cover, s0 + cnt))
            sel = _select_rows(_tag_tokens(xw_ref[:, d:]), ok, first_token, tm)
            acc_ref[...] += _dot(sel, xw_ref[:, :d])
            return carry

        lax.fori_loop(1, (cnt + cover - 1) // cover, extra, 0)

    pre = alpha * x_ref[...] + g_ref[0] * acc_ref[...]
    dl = pre - jnp.mean(pre, axis=-1, keepdims=True)
    var = jnp.mean(dl * dl, axis=-1, keepdims=True)
    y = dl * lax.rsqrt(var + LN_EPS) * lg_ref[...] + lb_ref[...]
    x2_ref[...] = y
    h_ref[...] = (y * (1.0 + sc_ref[0]) + sh_ref[0]).astype(BF16)


def _combine_ln2(x, oe, s0, cnt, gate, ln_g, ln_b, sc, sh, seq_len, tm, alpha, n_exp):
    t, d = x.shape
    total_rows = oe.shape[0]
    win = COMBINE_WINDOW
    assert total_rows >= win and total_rows % BF16_ROW_TILE == 0 and t <= TAG_BASE * 256
    per = seq_len // tm
    tile = pl.BlockSpec((tm, d), lambda i, *_: (i, 0))
    row = pl.BlockSpec((1, d), lambda i, *_: (0, 0))
    vec = pl.BlockSpec((1, 1, d), lambda i, *_: (i // per, 0, 0))
    return pl.pallas_call(
        functools.partial(_combine_ln2_kernel, alpha=alpha, n_exp=n_exp, win=win, tm=tm,
                          total_rows=total_rows),
        out_shape=(jax.ShapeDtypeStruct((t, d), F32), jax.ShapeDtypeStruct((t, d), BF16)),
        grid_spec=pltpu.PrefetchScalarGridSpec(
            num_scalar_prefetch=2,
            grid=(t // tm,),
            in_specs=[tile, pl.BlockSpec(memory_space=pl.ANY), vec, row, row, vec, vec],
            out_specs=(tile, tile),
            scratch_shapes=[pltpu.VMEM((2, n_exp * win, d + TAG_LANES), BF16),
                            pltpu.VMEM((win, d + TAG_LANES), BF16),
                            pltpu.VMEM((tm, d), F32),
                            pltpu.SemaphoreType.DMA((2, n_exp)),
                            pltpu.SemaphoreType.DMA((1,))]),
        compiler_params=_cparams("arbitrary"),
        name="combine_ln2",
    )(s0, cnt, x, oe, gate, ln_g, ln_b, sc, sh)


def _mixer_inputs(h, lp, bs, ls, h0):
    proj = _proj(h, lp['w_in'], lp['layer'])
    yf, yb, hfin = _s5(proj.reshape(bs, ls, -1), h0, lp['s5_ops'], col_block=2)
    return proj, yf.reshape(bs * ls, BRANCH), yb.reshape(bs * ls, BRANCH), hfin


def _expert_choice(h2, aff, lp, bs, ls, tm):
    n_exp = lp['w1'].shape[1]
    d = h2.shape[1]
    cap = CAPACITY_FACTOR * ls // n_exp
    m = bs * cap
    a = aff[:, :n_exp].reshape(bs, ls, n_exp)
    top_w, top_idx = lax.top_k(jnp.swapaxes(a, 1, 2), cap)
    top_idx, top_w = lax.sort((top_idx, top_w), dimension=2, num_keys=1)
    gidx = top_idx + (jnp.arange(bs, dtype=top_idx.dtype) * ls)[:, None, None]
    gidx = gidx.transpose(1, 0, 2).reshape(-1)
    xs = h2.at[gidx].get(mode='promise_in_bounds').reshape(n_exp, m, d)
    tw = top_w.transpose(1, 0, 2).reshape(n_exp, m, 1)
    tag = jnp.stack([gidx // TAG_BASE, gidx % TAG_BASE], axis=-1).astype(BF16)
    tag = jnp.pad(tag, ((0, 0), (0, TAG_LANES - 2))).reshape(n_exp, m, TAG_LANES)
    oe = _expert_ffn(xs, lp['w1'], lp['w3'], lp['w2'], tw, tag, lp['layer'])
    tile_start = jnp.arange(ls // tm, dtype=top_idx.dtype) * tm
    below = jnp.sum(top_idx[:, :, None, :] < tile_start[None, None, :, None], axis=-1)
    upto = jnp.concatenate([below[:, :, 1:], jnp.full((bs, n_exp, 1), cap, below.dtype)], axis=2)
    base = (jnp.arange(n_exp) * m)[None, :, None] + (jnp.arange(bs) * cap)[:, None, None]
    s0 = (base + below).transpose(0, 2, 1).reshape(-1).astype(jnp.int32)
    cnt = (upto - below).transpose(0, 2, 1).reshape(-1).astype(jnp.int32)
    return oe.reshape(n_exp * m, d + TAG_LANES), s0, cnt


def _layer(x, h, mod, nxt, lp, bs, ls, seg, h0, alpha):
    proj, yf, yb, hfin = _mixer_inputs(h, lp, bs, ls, h0)
    yabc = _mixers(proj, yf, yb, lp, seg)
    merged = _gate_merge(h, yabc, lp['wg'], lp['bg'], lp['wb'], lp['layer'])
    n_exp = lp['w1'].shape[1]
    x1, h2, aff = _out_ln(merged, lp['w_out'], x, mod[2], lp['ln1_g'], lp['ln1_b'], mod[4], mod[3],
                          lp['w_router'], lp['layer'], ls, alpha, n_exp)
    tm = min(256, ls)
    oe, s0, cnt = _expert_choice(h2, aff, lp, bs, ls, tm)
    x2, hn = _combine_ln2(x1, oe, s0, cnt, mod[5], lp['ln2_g'], lp['ln2_b'], nxt[1], nxt[0],
                          ls, tm, alpha, n_exp)
    return x2, hn, hfin


def kernel(x, c, ctx, c_ctx, ada_w_down, ada_w_up, ada_b, w_in, gmlp_w_s, gmlp_b_s, s5_a_re, s5_a_im, s5_log_step, s5_b_re, s5_b_im, s5_c_re, s5_c_im, s5_d, s5_w_glu, s5_b_glu, pool_w, pool_scale, w_gate, b_gate, w_branch, w_out, ln1_g, ln1_b, w_router, w1, w3, w2, ln2_g, ln2_b):
    bs, n, d = x.shape
    lc = ctx.shape[1]
    depth = w_in.shape[0]
    n_exp = w_router.shape[2]
    alpha = (2.0 * depth) ** 0.25
    assert bs + 1 <= SUBLANES and n % GRID_W == 0

    cond = jnp.zeros((SUBLANES, d), F32).at[:bs].set(c).at[bs].set(c_ctx)
    mods = _ada(cond, ada_w_down, ada_w_up, ada_b).reshape(depth, SUBLANES, N_MOD, d)

    def lat_mod(l):
        return mods[l, :bs].transpose(1, 0, 2)[:, :, None, :]

    def ctx_mod(l):
        return jnp.broadcast_to(mods[l, bs][:, None, None, :], (N_MOD, bs, 1, d))

    wg_b, wb_b, wo_b = w_gate.astype(BF16), w_branch.astype(BF16), w_out.astype(BF16)
    w1_b, w3_b, w2_b = w1.astype(BF16), w3.astype(BF16), w2.astype(BF16)

    def layer_params(l):
        return dict(
            layer=l, w_in=w_in,
            s5_ops=_s5_operators(s5_a_re[l], s5_a_im[l], s5_log_step[l], s5_b_re[l], s5_b_im[l],
                                 s5_c_re[l], s5_c_im[l]),
            ws=gmlp_w_s[l].astype(BF16),
            bs=jnp.broadcast_to(gmlp_b_s[l][:, :, None], (HEADS, CHUNK, 128)).astype(F32),
            sd=s5_d[l].reshape(1, BRANCH), wglu=s5_w_glu[l].astype(BF16), bglu=s5_b_glu[l].reshape(1, BRANCH),
            pw=pool_w[l].astype(BF16), ps=pool_scale[l].reshape(1, BRANCH),
            wg=wg_b, bg=b_gate[l].reshape(3, 1, d), wb=wb_b,
            w_out=wo_b, ln1_g=ln1_g[l].reshape(1, d), ln1_b=ln1_b[l].reshape(1, d),
            w_router=jnp.zeros((d, ROUTER_LANES), BF16).at[:, :n_exp].set(w_router[l].astype(BF16)),
            w1=w1_b, w3=w3_b, w2=w2_b,
            ln2_g=ln2_g[l].reshape(1, d), ln2_b=ln2_b[l].reshape(1, d))

    xl = x.reshape(bs * n, d)
    xc = ctx.reshape(bs * lc, d)
    m0, c0 = lat_mod(0), ctx_mod(0)
    hl = _modulate(xl, m0[0], m0[1], n)
    hc = _modulate(xc, c0[0], c0[1], lc)
    zero_state = jnp.zeros((S5_GB, 2, SUBLANES, S5_GB * S5_STATE), F32)
    for l in range(depth):
        lp = layer_params(l)
        nl = min(l + 1, depth - 1)
        if l < depth - 1:
            xc, hc, ctx_final = _layer(xc, hc, ctx_mod(l), ctx_mod(nl), lp, bs, lc, lc, zero_state, alpha)
        else:
            ctx_final = _mixer_inputs(hc, lp, bs, lc, zero_state)[-1]
        xl, hl, _ = _layer(xl, hl, lat_mod(l), lat_mod(nl), lp, bs, n, GRID_W, ctx_final, alpha)
    return xl.reshape(bs, n, d)
```

```python
import functools
import math

import jax
import jax.numpy as jnp
from jax import lax
from jax.experimental import pallas as pl
from jax.experimental.pallas import tpu as pltpu

F32 = jnp.float32
BF16 = jnp.bfloat16

LN_EPS = 1e-6
GRID_W = 64
CHUNK = 128
BRANCH = 1024
HEADS = 8
S5_GROUP = 16
S5_STATE = 64
S5_GB = 8
POOL_WINDOWS = (2, 4, 8, 16)
POOL_GROUP = BRANCH // len(POOL_WINDOWS)
N_MOD = 6
ROUTER_LANES = 128
CAPACITY_FACTOR = 2
TAG_LANES = 128
TAG_BASE = 128
COMBINE_WINDOW = 64
COLUMN_BLOCK = 512

V7X_VMEM_LIMIT = 60 * 1024 * 1024
SUBLANES = 8
BF16_ROW_TILE = 16


def _cparams(*sem):
    return pltpu.CompilerParams(dimension_semantics=sem, vmem_limit_bytes=V7X_VMEM_LIMIT)


def _sigmoid(x):
    return 1.0 / (1.0 + jnp.exp(-x))


def _gelu(x):
    c = math.sqrt(2.0 / math.pi)
    return x * (0.5 * (1.0 + jnp.tanh(c * (x + 0.044715 * (x * x * x)))))


def _dot(a, b):
    return jnp.dot(a, b, preferred_element_type=F32)


def _ada_down_kernel(c_ref, w_ref, o_ref):
    @pl.when(pl.program_id(1) == 0)
    def _():
        o_ref[...] = jnp.zeros_like(o_ref)

    c = c_ref[...]
    o_ref[0] += _dot((c * _sigmoid(c)).astype(BF16), w_ref[0].astype(BF16))


def _ada_up_kernel(r_ref, w_ref, b_ref, o_ref):
    o_ref[0] = _dot(r_ref[0].astype(BF16), w_ref[0].astype(BF16)) + b_ref[0]


def _ada(cond, w_down, w_up, b_up):
    depth, d, rank = w_down.shape
    nmod = w_up.shape[2]
    tk, tn = 1024, 2048
    r = pl.pallas_call(
        _ada_down_kernel,
        out_shape=jax.ShapeDtypeStruct((depth, SUBLANES, rank), F32),
        grid=(depth, d // tk),
        in_specs=[pl.BlockSpec((SUBLANES, tk), lambda l, k: (0, k)),
                  pl.BlockSpec((1, tk, rank), lambda l, k: (l, k, 0))],
        out_specs=pl.BlockSpec((1, SUBLANES, rank), lambda l, k: (l, 0, 0)),
        compiler_params=_cparams("parallel", "arbitrary"),
        name="ada_down",
    )(cond, w_down)
    return pl.pallas_call(
        _ada_up_kernel,
        out_shape=jax.ShapeDtypeStruct((depth, SUBLANES, nmod), F32),
        grid=(depth, nmod // tn),
        in_specs=[pl.BlockSpec((1, SUBLANES, rank), lambda l, n: (l, 0, 0)),
                  pl.BlockSpec((1, rank, tn), lambda l, n: (l, 0, n)),
                  pl.BlockSpec((1, 1, tn), lambda l, n: (l, 0, n))],
        out_specs=pl.BlockSpec((1, SUBLANES, tn), lambda l, n: (l, 0, n)),
        compiler_params=_cparams("parallel", "parallel"),
        name="ada_up",
    )(r, w_up, b_up.reshape(depth, 1, nmod))


def _modulate_kernel(x_ref, sh_ref, sc_ref, o_ref):
    o_ref[...] = (x_ref[...] * (1.0 + sc_ref[0]) + sh_ref[0]).astype(BF16)


def _modulate(x, sh, sc, seq_len):
    t, d = x.shape
    tm = min(512, seq_len)
    per = seq_len // tm
    vec = pl.BlockSpec((1, 1, d), lambda i: (i // per, 0, 0))
    return pl.pallas_call(
        _modulate_kernel,
        out_shape=jax.ShapeDtypeStruct((t, d), BF16),
        grid=(t // tm,),
        in_specs=[pl.BlockSpec((tm, d), lambda i: (i, 0)), vec, vec],
        out_specs=pl.BlockSpec((tm, d), lambda i: (i, 0)),
        compiler_params=_cparams("parallel"),
        name="modulate",
    )(x, sh, sc)


def _proj_kernel(a_ref, w_ref, o_ref, wb_ref):
    j = pl.program_id(0)

    @pl.when(pl.program_id(1) == 0)
    def _():
        wb_ref[...] = w_ref[...].astype(BF16)

    acc = _dot(a_ref[...], wb_ref[...])

    @pl.when(j == 0)
    def _():
        o_ref[...] = _gelu(acc).astype(BF16)

    @pl.when(j == 1)
    def _():
        g = _gelu(acc)
        dlt = g - jnp.mean(g, axis=-1, keepdims=True)
        var = jnp.mean(dlt * dlt, axis=-1, keepdims=True)
        o_ref[...] = (dlt * lax.rsqrt(var + LN_EPS)).astype(BF16)

    @pl.when(j >= 2)
    def _():
        o_ref[...] = acc.astype(BF16)


def _proj(h, w_in, layer):
    t, d = h.shape
    n = w_in.shape[2]
    tm = min(512, t)
    return pl.pallas_call(
        _proj_kernel,
        out_shape=jax.ShapeDtypeStruct((t, n), BF16),
        grid=(n // BRANCH, t // tm),
        in_specs=[pl.BlockSpec((tm, d), lambda j, i: (i, 0)),
                  pl.BlockSpec((None, d, BRANCH), lambda j, i: (layer, 0, j))],
        out_specs=pl.BlockSpec((tm, BRANCH), lambda j, i: (i, j)),
        scratch_shapes=[pltpu.VMEM((d, BRANCH), BF16)],
        compiler_params=_cparams("parallel", "arbitrary"),
        name="proj",
    )(h, w_in)


def _s5_kernel(uf_ref, ub_ref, pf_ref, pb_ref, pft_ref, pbt_ref, bm_ref, cm_ref, lam_ref, h0_ref,
               yf_ref, yb_ref, hfin_ref, uf_s, ub_s, of_s, ob_s, bu_ref, st_ref, *, tblk, nseq):
    i = pl.program_id(0)
    half = S5_GB * S5_STATE
    src_rows = nseq * tblk

    @pl.when(i == 0)
    def _():
        st_ref[...] = h0_ref[...]

    uf_s[...] = _dot(pf_ref[...], uf_ref[...].reshape(src_rows, BRANCH)).astype(BF16)
    ub_s[...] = _dot(pb_ref[...], ub_ref[...].reshape(src_rows, BRANCH)).astype(BF16)

    for gb in range(S5_GB):
        cols = slice(gb * 128, (gb + 1) * 128)
        bu = bu_ref.at[gb % 2]
        bu[...] = _dot(jnp.concatenate([uf_s[:, cols], ub_s[:, cols]], axis=1), bm_ref[gb])
        lr = lam_ref[gb, 0]
        li = lam_ref[gb, 1]

        def step(t, carry, bu=bu, lr=lr, li=li):
            hr, hi = carry
            r0 = pl.multiple_of(t * SUBLANES, SUBLANES)
            nr = lr * hr - li * hi + bu[pl.ds(r0, SUBLANES), 0:half]
            ni = lr * hi + li * hr + bu[pl.ds(r0, SUBLANES), half:2 * half]
            bu[pl.ds(r0, SUBLANES), 0:half] = nr
            bu[pl.ds(r0, SUBLANES), half:2 * half] = ni
            return nr, ni

        hr, hi = lax.fori_loop(0, tblk, step, (st_ref[gb, 0], st_ref[gb, 1]), unroll=True)
        st_ref[gb, 0] = hr
        st_ref[gb, 1] = hi
        yy = _dot(bu[...].astype(BF16), cm_ref[gb]).astype(BF16)
        of_s[:, cols] = yy[:, :128]
        ob_s[:, cols] = yy[:, 128:]

    yf_ref[...] = _dot(pft_ref[...], of_s[...]).astype(BF16).reshape(nseq, tblk, BRANCH)
    yb_ref[...] = _dot(pbt_ref[...], ob_s[...]).astype(BF16).reshape(nseq, tblk, BRANCH)

    @pl.when(i == pl.num_programs(0) - 1)
    def _():
        hfin_ref[...] = st_ref[...]


def _s5_operators(a_re, a_im, log_step, b_re, b_im, c_re, c_im):
    a_re = a_re.astype(F32)
    a_im = a_im.astype(F32)
    step = jnp.exp(log_step.astype(F32))[..., None]
    mag = jnp.exp(a_re * step)
    lam_re = mag * jnp.cos(a_im * step)
    lam_im = mag * jnp.sin(a_im * step)
    den = a_re * a_re + a_im * a_im
    f_re = ((lam_re - 1.0) * a_re + lam_im * a_im) / den
    f_im = (lam_im * a_re - (lam_re - 1.0) * a_im) / den
    b_re = b_re.astype(F32)[None]
    b_im = b_im.astype(F32)[None]
    bb_re = f_re[..., None] * b_re - f_im[..., None] * b_im
    bb_im = f_re[..., None] * b_im + f_im[..., None] * b_re
    ngl = 128 // S5_GROUP
    eye = jnp.eye(ngl, dtype=F32)
    bb = jnp.stack([bb_re, bb_im], axis=1)
    bb = bb.reshape(2, 2, S5_GB, ngl, S5_STATE, S5_GROUP)
    bmat = jnp.einsum('drgaph,ab->gdahrbp', bb, eye).reshape(S5_GB, 2 * 128, 2 * ngl * S5_STATE)
    cc = jnp.stack([c_re.astype(F32), -c_im.astype(F32)], axis=1)
    cc = cc.reshape(2, 2, S5_GB, ngl, S5_GROUP, S5_STATE)
    cmat = jnp.einsum('drgahp,ab->grapdbh', cc, eye).reshape(S5_GB, 2 * ngl * S5_STATE, 2 * 128)
    lam = jnp.stack([lam_re, lam_im], axis=1)
    lam = lam.reshape(2, 2, S5_GB, ngl * S5_STATE).transpose(2, 1, 0, 3)
    lam = jnp.repeat(lam, SUBLANES // 2, axis=2)
    return bmat.astype(BF16), cmat.astype(BF16), lam


def _scan_order(nseq, tblk):
    step = jnp.arange(tblk * SUBLANES) // SUBLANES
    seq = jnp.arange(tblk * SUBLANES) % SUBLANES
    src = jnp.arange(nseq * tblk)[None, :]
    fwd = (seq < nseq)[:, None] & (src == (seq * tblk + step)[:, None])
    bwd = (seq >= nseq)[:, None] & (src == ((seq - nseq) * tblk + (tblk - 1 - step))[:, None])
    return fwd.astype(BF16), bwd.astype(BF16), fwd.T.astype(BF16), bwd.T.astype(BF16)


def _s5(proj, h0, ops, col_block):
    bs, ls, _ = proj.shape
    assert 2 * bs == SUBLANES
    bmat, cmat, lam = ops
    tblk = 64
    nblk = ls // tblk
    rows = tblk * SUBLANES
    perms = _scan_order(bs, tblk)
    st_shape = (S5_GB, 2, SUBLANES, S5_GB * S5_STATE)
    full = lambda shape: pl.BlockSpec(shape, lambda i: (0,) * len(shape))
    y_shape = jax.ShapeDtypeStruct((bs, ls, BRANCH), BF16)
    return pl.pallas_call(
        functools.partial(_s5_kernel, tblk=tblk, nseq=bs),
        out_shape=(y_shape, y_shape, jax.ShapeDtypeStruct(st_shape, F32)),
        grid=(nblk,),
        in_specs=[pl.BlockSpec((bs, tblk, BRANCH), lambda i: (0, i, col_block)),
                  pl.BlockSpec((bs, tblk, BRANCH), lambda i: (0, nblk - 1 - i, col_block))]
                 + [full(p.shape) for p in perms]
                 + [full(bmat.shape), full(cmat.shape), full(lam.shape), full(st_shape)],
        out_specs=(pl.BlockSpec((bs, tblk, BRANCH), lambda i: (0, i, 0)),
                   pl.BlockSpec((bs, tblk, BRANCH), lambda i: (0, nblk - 1 - i, 0)),
                   full(st_shape)),
        scratch_shapes=[pltpu.VMEM((rows, BRANCH), BF16)] * 4
                       + [pltpu.VMEM((2, rows, 2 * S5_GB * S5_STATE), F32), pltpu.VMEM(st_shape, F32)],
        compiler_params=_cparams("arbitrary"),
        name="s5_scan",
    )(proj, proj, *perms, bmat, cmat, lam, h0)


def _mixers_kernel(p_ref, yf_ref, yb_ref, ws_ref, bs_ref, sd_ref, wglu_ref, bglu_ref,
                   pm_ref, ic_ref, pw_ref, ps_ref, o_ref, *, tm):
    for ch in range(tm // CHUNK):
        r = slice(ch * CHUNK, (ch + 1) * CHUNK)
        for h in range(HEADS):
            vn = p_ref[r, BRANCH + h * 128:BRANCH + (h + 1) * 128]
            mixed = _dot(ws_ref[h], vn) + bs_ref[h]
            o_ref[r, h * 128:(h + 1) * 128] = (p_ref[r, h * 128:(h + 1) * 128].astype(F32) * mixed).astype(BF16)
    y = (yf_ref[...].astype(F32) + yb_ref[...].astype(F32)
         + p_ref[:, 2 * BRANCH:3 * BRANCH].astype(F32) * sd_ref[...])
    g = _gelu(y)
    z = _dot(g.astype(BF16), wglu_ref[...]) + bglu_ref[...]
    o_ref[:, BRANCH:2 * BRANCH] = (g * _sigmoid(z)).astype(BF16)
    for w in range(len(POOL_WINDOWS)):
        c0 = 3 * BRANCH + w * POOL_GROUP
        gq = p_ref[:, c0:c0 + POOL_GROUP]
        s = _dot(pm_ref[w], gq)
        yv = s * ic_ref[w] - gq.astype(F32)
        yc = _dot(yv.astype(BF16), pw_ref[w]) * ps_ref[:, w * POOL_GROUP:(w + 1) * POOL_GROUP]
        o0 = 2 * BRANCH + w * POOL_GROUP
        o_ref[:, o0:o0 + POOL_GROUP] = yc.astype(BF16)


def _pool_operators(tm, seg):
    t = jnp.arange(tm)
    base = (t // seg) * seg
    pos = t - base
    mats, invs = [], []
    for k in POOL_WINDOWS:
        lo = base + jnp.clip(pos - k // 2, 0, seg - 1)
        hi = base + jnp.clip(pos + k // 2 - 1, 0, seg - 1)
        mats.append(((t[None, :] >= lo[:, None]) & (t[None, :] <= hi[:, None])).astype(BF16))
        cnt = (hi - lo + 1).astype(F32)
        invs.append(jnp.broadcast_to((1.0 / cnt)[:, None], (tm, POOL_GROUP)))
    return jnp.stack(mats), jnp.stack(invs)


def _mixers(proj, yf, yb, prm, seg):
    t, n = proj.shape
    tm = 256
    assert tm % seg == 0 and tm % CHUNK == 0
    pm, ic = _pool_operators(tm, seg)
    full = lambda a: pl.BlockSpec(a.shape, lambda i: (0,) * a.ndim)
    ws, bs, sd, wglu, bglu, pw, ps = (prm[k] for k in ('ws', 'bs', 'sd', 'wglu', 'bglu', 'pw', 'ps'))
    return pl.pallas_call(
        functools.partial(_mixers_kernel, tm=tm),
        out_shape=jax.ShapeDtypeStruct((t, 3 * BRANCH), BF16),
        grid=(t // tm,),
        in_specs=[pl.BlockSpec((tm, n), lambda i: (i, 0)),
                  pl.BlockSpec((tm, BRANCH), lambda i: (i, 0)),
                  pl.BlockSpec((tm, BRANCH), lambda i: (i, 0)),
                  full(ws), full(bs), full(sd), full(wglu), full(bglu),
                  full(pm), full(ic), full(pw), full(ps)],
        out_specs=pl.BlockSpec((tm, 3 * BRANCH), lambda i: (i, 0)),
        compiler_params=_cparams("parallel"),
        name="mixers",
    )(proj, yf, yb, ws, bs, sd, wglu, bglu, pm, ic, pw, ps)


def _gate_merge_kernel(h_ref, y_ref, wg0_ref, wg1_ref, wg2_ref, bg_ref, wb_ref, o_ref):
    h = h_ref[...]
    acc = None
    for k, wg_ref in enumerate((wg0_ref, wg1_ref, wg2_ref)):
        gate = _sigmoid(_dot(h, wg_ref[...]) + bg_ref[k])
        term = gate * _dot(y_ref[:, k * BRANCH:(k + 1) * BRANCH], wb_ref[k])
        acc = term if acc is None else acc + term
    o_ref[...] = acc.astype(BF16)


def _gate_merge(h, yabc, w_gate, bg, wb, layer):
    t, d = h.shape
    nj, tn = w_gate.shape[2], w_gate.shape[4]
    tm = min(512, t)
    gate_cols = lambda k: pl.BlockSpec((None, None, None, d, tn), lambda j, i: (layer, k, j, 0, 0))
    return pl.pallas_call(
        _gate_merge_kernel,
        out_shape=jax.ShapeDtypeStruct((t, d), BF16),
        grid=(nj, t // tm),
        in_specs=[pl.BlockSpec((tm, d), lambda j, i: (i, 0)),
                  pl.BlockSpec((tm, 3 * BRANCH), lambda j, i: (i, 0)),
                  gate_cols(0), gate_cols(1), gate_cols(2),
                  pl.BlockSpec((3, 1, tn), lambda j, i: (0, 0, j)),
                  pl.BlockSpec((None, None, 3, BRANCH, tn), lambda j, i: (layer, j, 0, 0, 0))],
        out_specs=pl.BlockSpec((tm, tn), lambda j, i: (i, j)),
        compiler_params=_cparams("parallel", "parallel"),
        name="gate_merge",
    )(h, yabc, w_gate, w_gate, w_gate, bg, wb)


def _out_ln_kernel(m_ref, w_ref, x_ref, g_ref, lg_ref, lb_ref, sc_ref, sh_ref, wr_ref,
                   x1_ref, h2_ref, aff_ref, pre_ref, *, nj, tn, alpha, n_experts):
    j = pl.program_id(1)
    pre_ref[j] = alpha * x_ref[...] + g_ref[0] * _dot(m_ref[...], w_ref[...])

    @pl.when(j == nj - 1)
    def _():
        d = nj * tn
        tot = None
        for jj in range(nj):
            s = jnp.sum(pre_ref[jj], axis=-1, keepdims=True)
            tot = s if tot is None else tot + s
        mu = tot / d
        tot = None
        for jj in range(nj):
            dl = pre_ref[jj] - mu
            s = jnp.sum(dl * dl, axis=-1, keepdims=True)
            tot = s if tot is None else tot + s
        rstd = lax.rsqrt(tot / d + LN_EPS)
        for jj in range(nj):
            cs = slice(jj * tn, (jj + 1) * tn)
            y = (pre_ref[jj] - mu) * rstd * lg_ref[:, cs] + lb_ref[:, cs]
            x1_ref[:, cs] = y
            h2_ref[:, cs] = (y * (1.0 + sc_ref[0, :, cs]) + sh_ref[0, :, cs]).astype(BF16)
        logits = _dot(h2_ref[...], wr_ref[...])
        lane = lax.broadcasted_iota(jnp.int32, logits.shape, 1)
        logits = jnp.where(lane < n_experts, logits, -jnp.inf)
        e = jnp.exp(logits - jnp.max(logits, axis=-1, keepdims=True))
        aff_ref[...] = e / jnp.sum(e, axis=-1, keepdims=True)


def _out_ln(merged, w_out, x, gate, ln_g, ln_b, sc2, sh2, w_router, layer, seq_len, alpha, n_experts):
    t, d = x.shape
    nj, tn = w_out.shape[1], w_out.shape[3]
    tm = min(512, seq_len)
    per = seq_len // tm
    row = pl.BlockSpec((1, d), lambda i, j: (0, 0))
    vec = pl.BlockSpec((1, 1, d), lambda i, j: (i // per, 0, 0))
    return pl.pallas_call(
        functools.partial(_out_ln_kernel, nj=nj, tn=tn, alpha=alpha, n_experts=n_experts),
        out_shape=(jax.ShapeDtypeStruct((t, d), F32),
                   jax.ShapeDtypeStruct((t, d), BF16),
                   jax.ShapeDtypeStruct((t, ROUTER_LANES), F32)),
        grid=(t // tm, nj),
        in_specs=[pl.BlockSpec((tm, d), lambda i, j: (i, 0)),
                  pl.BlockSpec((None, None, d, tn), lambda i, j: (layer, j, 0, 0)),
                  pl.BlockSpec((tm, tn), lambda i, j: (i, j)),
                  pl.BlockSpec((1, 1, tn), lambda i, j: (i // per, 0, j)),
                  row, row, vec, vec,
                  pl.BlockSpec((d, ROUTER_LANES), lambda i, j: (0, 0))],
        out_specs=(pl.BlockSpec((tm, d), lambda i, j: (i, 0)),
                   pl.BlockSpec((tm, d), lambda i, j: (i, 0)),
                   pl.BlockSpec((tm, ROUTER_LANES), lambda i, j: (i, 0))),
        scratch_shapes=[pltpu.VMEM((nj, tm, tn), F32)],
        compiler_params=_cparams("parallel", "arbitrary"),
        name="out_ln",
    )(merged, w_out, x, gate, ln_g, ln_b, sc2, sh2, w_router)


def _ffn_kernel(x_ref, w1_ref, w3_ref, w2_ref, tw_ref, tag_ref, o_ref):
    d = x_ref.shape[2]
    x = x_ref[0]
    h1 = _dot(x, w1_ref[0])
    hid = (h1 * _sigmoid(h1)) * _dot(x, w3_ref[0])
    o_ref[0, :, :d] = (_dot(hid.astype(BF16), w2_ref[0]) * tw_ref[0]).astype(BF16)
    o_ref[0, :, d:] = tag_ref[0]


def _expert_ffn(xs, w1, w3, w2, tw, tag, layer):
    e, m, d = xs.shape
    ff = w1.shape[3]
    tm = min(512, m)
    return pl.pallas_call(
        _ffn_kernel,
        out_shape=jax.ShapeDtypeStruct((e, m, d + TAG_LANES), BF16),
        grid=(e, m // tm),
        in_specs=[pl.BlockSpec((1, tm, d), lambda ei, i: (ei, i, 0)),
                  pl.BlockSpec((None, 1, d, ff), lambda ei, i: (layer, ei, 0, 0)),
                  pl.BlockSpec((None, 1, d, ff), lambda ei, i: (layer, ei, 0, 0)),
                  pl.BlockSpec((None, 1, ff, d), lambda ei, i: (layer, ei, 0, 0)),
                  pl.BlockSpec((1, tm, 1), lambda ei, i: (ei, i, 0)),
                  pl.BlockSpec((1, tm, TAG_LANES), lambda ei, i: (ei, i, 0))],
        out_specs=pl.BlockSpec((1, tm, d + TAG_LANES), lambda ei, i: (ei, i, 0)),
        compiler_params=_cparams("parallel", "parallel"),
        name="expert_ffn",
    )(xs, w1, w3, w2, tw, tag)


def _tag_tokens(tags):
    lane = lax.broadcasted_iota(jnp.int32, (SUBLANES, TAG_LANES), 1)
    coef = jnp.where(lane == 0, float(TAG_BASE), jnp.where(lane == 1, 1.0, 0.0)).astype(BF16)
    return lax.dot_general(coef, tags, (((1,), (1,)), ((), ())), preferred_element_type=F32)[0:1]


def _select_rows(tok, valid, first_token, tm):
    ids = (first_token + lax.broadcasted_iota(jnp.int32, (tm, tok.shape[1]), 0)).astype(F32)
    return jnp.where(jnp.where(valid, tok, -1.0) == ids, 1.0, 0.0).astype(BF16)


def _combine_ln2_kernel(s0_ref, cnt_ref, x_ref, oe_ref, g_ref, lg_ref, lb_ref, sc_ref, sh_ref,
                        x2_ref, h_ref, win_ref, xw_ref, acc_ref, sem, xsem,
                        *, alpha, n_exp, win, tm, total_rows):
    i = pl.program_id(0)
    d = x_ref.shape[1]
    slot = i % 2
    last_start = total_rows - win
    cover = win - BF16_ROW_TILE
    first_token = i * tm

    def window_start(first_row):
        start = (jnp.minimum(first_row, last_start) // BF16_ROW_TILE) * BF16_ROW_TILE
        return pl.multiple_of(start, BF16_ROW_TILE)

    def window_copy(start, buf, e):
        return pltpu.make_async_copy(oe_ref.at[pl.ds(start, win)],
                                     win_ref.at[buf, pl.ds(e * win, win)], sem.at[buf, e])

    def fetch(tile, buf):
        for e in range(n_exp):
            window_copy(window_start(s0_ref[tile * n_exp + e]), buf, e).start()

    @pl.when(i == 0)
    def _():
        fetch(0, 0)

    @pl.when(i + 1 < pl.num_programs(0))
    def _():
        fetch(i + 1, 1 - slot)

    for e in range(n_exp):
        window_copy(0, slot, e).wait()

    wl = lax.broadcasted_iota(jnp.int32, (1, n_exp * win), 1)
    row = jnp.zeros_like(wl)
    lo = jnp.zeros_like(wl)
    hi = jnp.zeros_like(wl)
    for e in range(n_exp):
        s0 = s0_ref[i * n_exp + e]
        cnt = cnt_ref[i * n_exp + e]
        in_e = (wl >= e * win) & (wl < (e + 1) * win)
        row = jnp.where(in_e, window_start(s0) + (wl - e * win), row)
        lo = jnp.where(in_e, s0, lo)
        hi = jnp.where(in_e, s0 + jnp.minimum(cnt, cover), hi)
    valid = (row >= lo) & (row < hi)
    tok = _tag_tokens(win_ref[slot, :, d:])
    acc_ref[...] = _dot(_select_rows(tok, valid, first_token, tm), win_ref[slot, :, :d])

    for e in range(n_exp):
        s0 = s0_ref[i * n_exp + e]
        cnt = cnt_ref[i * n_exp + e]

        def extra(k, carry, s0=s0, cnt=cnt):
            lo_k = s0 + k * cover
            start = window_start(lo_k)
            cp = pltpu.make_async_copy(oe_ref.at[pl.ds(start, win)], xw_ref, xsem.at[0])
            cp.start()
            cp.wait()
            row_k = start + lax.broadcasted_iota(jnp.int32, (1, win), 1)
            ok = (row_k >= lo_k) & (row_k < jnp.minimum(lo_k + cover, s0 + cnt))
            sel = _select_rows(_tag_tokens(xw_ref[:, d:]), ok, first_token, tm)
            acc_ref[...] += _dot(sel, xw_ref[:, :d])
            return carry

        lax.fori_loop(1, (cnt + cover - 1) // cover, extra, 0)

    pre = alpha * x_ref[...] + g_ref[0] * acc_ref[...]
    dl = pre - jnp.mean(pre, axis=-1, keepdims=True)
    var = jnp.mean(dl * dl, axis=-1, keepdims=True)
    y = dl * lax.rsqrt(var + LN_EPS) * lg_ref[...] + lb_ref[...]
    x2_ref[...] = y
    h_ref[...] = (y * (1.0 + sc_ref[0]) + sh_ref[0]).astype(BF16)


def _combine_ln2(x, oe, s0, cnt, gate, ln_g, ln_b, sc, sh, seq_len, tm, alpha, n_exp):
    t, d = x.shape
    total_rows = oe.shape[0]
    win = COMBINE_WINDOW
    assert total_rows >= win and total_rows % BF16_ROW_TILE == 0 and t <= TAG_BASE * 256
    per = seq_len // tm
    tile = pl.BlockSpec((tm, d), lambda i, *_: (i, 0))
    row = pl.BlockSpec((1, d), lambda i, *_: (0, 0))
    vec = pl.BlockSpec((1, 1, d), lambda i, *_: (i // per, 0, 0))
    return pl.pallas_call(
        functools.partial(_combine_ln2_kernel, alpha=alpha, n_exp=n_exp, win=win, tm=tm,
                          total_rows=total_rows),
        out_shape=(jax.ShapeDtypeStruct((t, d), F32), jax.ShapeDtypeStruct((t, d), BF16)),
        grid_spec=pltpu.PrefetchScalarGridSpec(
            num_scalar_prefetch=2,
            grid=(t // tm,),
            in_specs=[tile, pl.BlockSpec(memory_space=pl.ANY), vec, row, row, vec, vec],
            out_specs=(tile, tile),
            scratch_shapes=[pltpu.VMEM((2, n_exp * win, d + TAG_LANES), BF16),
                            pltpu.VMEM((win, d + TAG_LANES), BF16),
                            pltpu.VMEM((tm, d), F32),
                            pltpu.SemaphoreType.DMA((2, n_exp)),
                            pltpu.SemaphoreType.DMA((1,))]),
        compiler_params=_cparams("arbitrary"),
        name="combine_ln2",
    )(s0, cnt, x, oe, gate, ln_g, ln_b, sc, sh)


def _mixer_inputs(h, lp, bs, ls, h0):
    proj = _proj(h, lp['w_in'], lp['layer'])
    yf, yb, hfin = _s5(proj.reshape(bs, ls, -1), h0, lp['s5_ops'], col_block=2)
    return proj, yf.reshape(bs * ls, BRANCH), yb.reshape(bs * ls, BRANCH), hfin


def _expert_choice(h2, aff, lp, bs, ls, tm):
    n_exp = lp['w1'].shape[1]
    d = h2.shape[1]
    cap = CAPACITY_FACTOR * ls // n_exp
    m = bs * cap
    a = aff[:, :n_exp].reshape(bs, ls, n_exp)
    top_w, top_idx = lax.top_k(jnp.swapaxes(a, 1, 2), cap)
    top_idx, top_w = lax.sort((top_idx, top_w), dimension=2, num_keys=1)
    gidx = top_idx + (jnp.arange(bs, dtype=top_idx.dtype) * ls)[:, None, None]
    gidx = gidx.transpose(1, 0, 2).reshape(-1)
    xs = h2.at[gidx].get(mode='promise_in_bounds').reshape(n_exp, m, d)
    tw = top_w.transpose(1, 0, 2).reshape(n_exp, m, 1)
    tag = jnp.stack([gidx // TAG_BASE, gidx % TAG_BASE], axis=-1).astype(BF16)
    tag = jnp.pad(tag, ((0, 0), (0, TAG_LANES - 2))).reshape(n_exp, m, TAG_LANES)
    oe = _expert_ffn(xs, lp['w1'], lp['w3'], lp['w2'], tw, tag, lp['layer'])
    tile_start = jnp.arange(ls // tm, dtype=top_idx.dtype) * tm
    below = jnp.sum(top_idx[:, :, None, :] < tile_start[None, None, :, None], axis=-1)
    upto = jnp.concatenate([below[:, :, 1:], jnp.full((bs, n_exp, 1), cap, below.dtype)], axis=2)
    base = (jnp.arange(n_exp) * m)[None, :, None] + (jnp.arange(bs) * cap)[:, None, None]
    s0 = (base + below).transpose(0, 2, 1).reshape(-1).astype(jnp.int32)
    cnt = (upto - below).transpose(0, 2, 1).reshape(-1).astype(jnp.int32)
    return oe.reshape(n_exp * m, d + TAG_LANES), s0, cnt


def _layer(x, h, mod, nxt, lp, bs, ls, seg, h0, alpha):
    proj, yf, yb, hfin = _mixer_inputs(h, lp, bs, ls, h0)
    yabc = _mixers(proj, yf, yb, lp, seg)
    merged = _gate_merge(h, yabc, lp['wg'], lp['bg'], lp['wb'], lp['layer'])
    n_exp = lp['w1'].shape[1]
    x1, h2, aff = _out_ln(merged, lp['w_out'], x, mod[2], lp['ln1_g'], lp['ln1_b'], mod[4], mod[3],
                          lp['w_router'], lp['layer'], ls, alpha, n_exp)
    tm = min(256, ls)
    oe, s0, cnt = _expert_choice(h2, aff, lp, bs, ls, tm)
    x2, hn = _combine_ln2(x1, oe, s0, cnt, mod[5], lp['ln2_g'], lp['ln2_b'], nxt[1], nxt[0],
                          ls, tm, alpha, n_exp)
    return x2, hn, hfin


def kernel(x, c, ctx, c_ctx, ada_w_down, ada_w_up, ada_b, w_in, gmlp_w_s, gmlp_b_s, s5_a_re, s5_a_im, s5_log_step, s5_b_re, s5_b_im, s5_c_re, s5_c_im, s5_d, s5_w_glu, s5_b_glu, pool_w, pool_scale, w_gate, b_gate, w_branch, w_out, ln1_g, ln1_b, w_router, w1, w3, w2, ln2_g, ln2_b):
    bs, n, d = x.shape
    lc = ctx.shape[1]
    depth = w_in.shape[0]
    n_exp = w_router.shape[2]
    alpha = (2.0 * depth) ** 0.25
    assert bs + 1 <= SUBLANES and n % GRID_W == 0

    cond = jnp.zeros((SUBLANES, d), F32).at[:bs].set(c).at[bs].set(c_ctx)
    mods = _ada(cond, ada_w_down, ada_w_up, ada_b).reshape(depth, SUBLANES, N_MOD, d)

    def lat_mod(l):
        return mods[l, :bs].transpose(1, 0, 2)[:, :, None, :]

    def ctx_mod(l):
        return jnp.broadcast_to(mods[l, bs][:, None, None, :], (N_MOD, bs, 1, d))

    tn = COLUMN_BLOCK
    nj = d // tn
    wg_b = w_gate.reshape(depth, d, 3, nj, tn).transpose(0, 2, 3, 1, 4).astype(BF16)
    wb_b = w_branch.reshape(depth, 3, BRANCH, nj, tn).transpose(0, 3, 1, 2, 4).astype(BF16)
    wo_b = w_out.reshape(depth, d, nj, tn).transpose(0, 2, 1, 3).astype(BF16)
    w1_b, w3_b, w2_b = w1.astype(BF16), w3.astype(BF16), w2.astype(BF16)

    def layer_params(l):
        return dict(
            layer=l, w_in=w_in,
            s5_ops=_s5_operators(s5_a_re[l], s5_a_im[l], s5_log_step[l], s5_b_re[l], s5_b_im[l],
                                 s5_c_re[l], s5_c_im[l]),
            ws=gmlp_w_s[l].astype(BF16),
            bs=jnp.broadcast_to(gmlp_b_s[l][:, :, None], (HEADS, CHUNK, 128)).astype(F32),
            sd=s5_d[l].reshape(1, BRANCH), wglu=s5_w_glu[l].astype(BF16), bglu=s5_b_glu[l].reshape(1, BRANCH),
            pw=pool_w[l].astype(BF16), ps=pool_scale[l].reshape(1, BRANCH),
            wg=wg_b, bg=b_gate[l].reshape(3, 1, d), wb=wb_b,
            w_out=wo_b, ln1_g=ln1_g[l].reshape(1, d), ln1_b=ln1_b[l].reshape(1, d),
            w_router=jnp.zeros((d, ROUTER_LANES), BF16).at[:, :n_exp].set(w_router[l].astype(BF16)),
            w1=w1_b, w3=w3_b, w2=w2_b,
            ln2_g=ln2_g[l].reshape(1, d), ln2_b=ln2_b[l].reshape(1, d))

    xl = x.reshape(bs * n, d)
    xc = ctx.reshape(bs * lc, d)
    m0, c0 = lat_mod(0), ctx_mod(0)
    hl = _modulate(xl, m0[0], m0[1], n)
    hc = _modulate(xc, c0[0], c0[1], lc)
    zero_state = jnp.zeros((S5_GB, 2, SUBLANES, S5_GB * S5_STATE), F32)
    for l in range(depth):
        lp = layer_params(l)
        nl = min(l + 1, depth - 1)
        if l < depth - 1:
            xc, hc, ctx_final = _layer(xc, hc, ctx_mod(l), ctx_mod(nl), lp, bs, lc, lc, zero_state, alpha)
        else:
            ctx_final = _mixer_inputs(hc, lp, bs, lc, zero_state)[-1]
        xl, hl, _ = _layer(xl, hl, lat_mod(l), lat_mod(nl), lp, bs, n, GRID_W, ctx_final, alpha)
    return xl.reshape(bs, n, d)
```

```python
import functools
import math

import jax
import jax.numpy as jnp
from jax import lax
from jax.experimental import pallas as pl
from jax.experimental.pallas import tpu as pltpu

F32 = jnp.float32
BF16 = jnp.bfloat16

LN_EPS = 1e-6
GRID_W = 64
CHUNK = 128
BRANCH = 1024
HEADS = 8
S5_GROUP = 16
S5_STATE = 64
S5_GB = 8
POOL_WINDOWS = (2, 4, 8, 16)
POOL_GROUP = BRANCH // len(POOL_WINDOWS)
N_MOD = 6
ROUTER_LANES = 128
CAPACITY_FACTOR = 2
TAG_LANES = 128
TAG_BASE = 128
COMBINE_WINDOW = 64
COLUMN_BLOCK = 512

V7X_VMEM_LIMIT = 60 * 1024 * 1024
SUBLANES = 8
BF16_ROW_TILE = 16


def _cparams(*sem):
    return pltpu.CompilerParams(dimension_semantics=sem, vmem_limit_bytes=V7X_VMEM_LIMIT)


def _sigmoid(x):
    return 1.0 / (1.0 + jnp.exp(-x))


def _gelu(x):
    c = math.sqrt(2.0 / math.pi)
    return x * (0.5 * (1.0 + jnp.tanh(c * (x + 0.044715 * (x * x * x)))))


def _dot(a, b):
    return jnp.dot(a, b, preferred_element_type=F32)


def _ada_down_kernel(c_ref, w_ref, o_ref):
    @pl.when(pl.program_id(1) == 0)
    def _():
        o_ref[...] = jnp.zeros_like(o_ref)

    c = c_ref[...]
    o_ref[0] += _dot((c * _sigmoid(c)).astype(BF16), w_ref[0].astype(BF16))


def _ada_up_kernel(r_ref, w_ref, b_ref, o_ref):
    o_ref[0] = _dot(r_ref[0].astype(BF16), w_ref[0].astype(BF16)) + b_ref[0]


def _ada(cond, w_down, w_up, b_up):
    depth, d, rank = w_down.shape
    nmod = w_up.shape[2]
    tk, tn = 1024, 2048
    r = pl.pallas_call(
        _ada_down_kernel,
        out_shape=jax.ShapeDtypeStruct((depth, SUBLANES, rank), F32),
        grid=(depth, d // tk),
        in_specs=[pl.BlockSpec((SUBLANES, tk), lambda l, k: (0, k)),
                  pl.BlockSpec((1, tk, rank), lambda l, k: (l, k, 0))],
        out_specs=pl.BlockSpec((1, SUBLANES, rank), lambda l, k: (l, 0, 0)),
        compiler_params=_cparams("parallel", "arbitrary"),
        name="ada_down",
    )(cond, w_down)
    return pl.pallas_call(
        _ada_up_kernel,
        out_shape=jax.ShapeDtypeStruct((depth, SUBLANES, nmod), F32),
        grid=(depth, nmod // tn),
        in_specs=[pl.BlockSpec((1, SUBLANES, rank), lambda l, n: (l, 0, 0)),
                  pl.BlockSpec((1, rank, tn), lambda l, n: (l, 0, n)),
                  pl.BlockSpec((1, 1, tn), lambda l, n: (l, 0, n))],
        out_specs=pl.BlockSpec((1, SUBLANES, tn), lambda l, n: (l, 0, n)),
        compiler_params=_cparams("parallel", "parallel"),
        name="ada_up",
    )(r, w_up, b_up.reshape(depth, 1, nmod))


def _block_columns_kernel(w_ref, o_ref, *, tn):
    for j in range(o_ref.shape[0]):
        o_ref[j] = w_ref[:, j * tn:(j + 1) * tn].astype(BF16)


def _block_columns(w, tn):
    depth, k, n = w.shape
    rk = 256
    return pl.pallas_call(
        functools.partial(_block_columns_kernel, tn=tn),
        out_shape=jax.ShapeDtypeStruct((depth, n // tn, k, tn), BF16),
        grid=(depth, k // rk),
        in_specs=[pl.BlockSpec((None, rk, n), lambda l, r: (l, r, 0))],
        out_specs=pl.BlockSpec((None, n // tn, rk, tn), lambda l, r: (l, 0, r, 0)),
        compiler_params=_cparams("parallel", "parallel"),
        name="block_columns",
    )(w)


def _modulate_kernel(x_ref, sh_ref, sc_ref, o_ref):
    o_ref[...] = (x_ref[...] * (1.0 + sc_ref[0]) + sh_ref[0]).astype(BF16)


def _modulate(x, sh, sc, seq_len):
    t, d = x.shape
    tm = min(512, seq_len)
    per = seq_len // tm
    vec = pl.BlockSpec((1, 1, d), lambda i: (i // per, 0, 0))
    return pl.pallas_call(
        _modulate_kernel,
        out_shape=jax.ShapeDtypeStruct((t, d), BF16),
        grid=(t // tm,),
        in_specs=[pl.BlockSpec((tm, d), lambda i: (i, 0)), vec, vec],
        out_specs=pl.BlockSpec((tm, d), lambda i: (i, 0)),
        compiler_params=_cparams("parallel"),
        name="modulate",
    )(x, sh, sc)


def _proj_kernel(a_ref, w_ref, o_ref, wb_ref):
    j = pl.program_id(0)

    @pl.when(pl.program_id(1) == 0)
    def _():
        wb_ref[...] = w_ref[...].astype(BF16)

    acc = _dot(a_ref[...], wb_ref[...])

    @pl.when(j == 0)
    def _():
        o_ref[...] = _gelu(acc).astype(BF16)

    @pl.when(j == 1)
    def _():
        g = _gelu(acc)
        dlt = g - jnp.mean(g, axis=-1, keepdims=True)
        var = jnp.mean(dlt * dlt, axis=-1, keepdims=True)
        o_ref[...] = (dlt * lax.rsqrt(var + LN_EPS)).astype(BF16)

    @pl.when(j >= 2)
    def _():
        o_ref[...] = acc.astype(BF16)


def _proj(h, w_in, layer):
    t, d = h.shape
    n = w_in.shape[2]
    tm = min(512, t)
    return pl.pallas_call(
        _proj_kernel,
        out_shape=jax.ShapeDtypeStruct((t, n), BF16),
        grid=(n // BRANCH, t // tm),
        in_specs=[pl.BlockSpec((tm, d), lambda j, i: (i, 0)),
                  pl.BlockSpec((None, d, BRANCH), lambda j, i: (layer, 0, j))],
        out_specs=pl.BlockSpec((tm, BRANCH), lambda j, i: (i, j)),
        scratch_shapes=[pltpu.VMEM((d, BRANCH), BF16)],
        compiler_params=_cparams("parallel", "arbitrary"),
        name="proj",
    )(h, w_in)


def _s5_kernel(uf_ref, ub_ref, pf_ref, pb_ref, pft_ref, pbt_ref, bm_ref, cm_ref, lam_ref, h0_ref,
               yf_ref, yb_ref, hfin_ref, uf_s, ub_s, of_s, ob_s, bu_ref, st_ref, *, tblk, nseq):
    i = pl.program_id(0)
    half = S5_GB * S5_STATE
    src_rows = nseq * tblk

    @pl.when(i == 0)
    def _():
        st_ref[...] = h0_ref[...]

    uf_s[...] = _dot(pf_ref[...], uf_ref[...].reshape(src_rows, BRANCH)).astype(BF16)
    ub_s[...] = _dot(pb_ref[...], ub_ref[...].reshape(src_rows, BRANCH)).astype(BF16)

    for gb in range(S5_GB):
        cols = slice(gb * 128, (gb + 1) * 128)
        bu = bu_ref.at[gb % 2]
        bu[...] = _dot(jnp.concatenate([uf_s[:, cols], ub_s[:, cols]], axis=1), bm_ref[gb])
        lr = lam_ref[gb, 0]
        li = lam_ref[gb, 1]

        def step(t, carry, bu=bu, lr=lr, li=li):
            hr, hi = carry
            r0 = pl.multiple_of(t * SUBLANES, SUBLANES)
            nr = lr * hr - li * hi + bu[pl.ds(r0, SUBLANES), 0:half]
            ni = lr * hi + li * hr + bu[pl.ds(r0, SUBLANES), half:2 * half]
            bu[pl.ds(r0, SUBLANES), 0:half] = nr
            bu[pl.ds(r0, SUBLANES), half:2 * half] = ni
            return nr, ni

        hr, hi = lax.fori_loop(0, tblk, step, (st_ref[gb, 0], st_ref[gb, 1]), unroll=True)
        st_ref[gb, 0] = hr
        st_ref[gb, 1] = hi
        yy = _dot(bu[...].astype(BF16), cm_ref[gb]).astype(BF16)
        of_s[:, cols] = yy[:, :128]
        ob_s[:, cols] = yy[:, 128:]

    yf_ref[...] = _dot(pft_ref[...], of_s[...]).astype(BF16).reshape(nseq, tblk, BRANCH)
    yb_ref[...] = _dot(pbt_ref[...], ob_s[...]).astype(BF16).reshape(nseq, tblk, BRANCH)

    @pl.when(i == pl.num_programs(0) - 1)
    def _():
        hfin_ref[...] = st_ref[...]


def _s5_operators(a_re, a_im, log_step, b_re, b_im, c_re, c_im):
    a_re = a_re.astype(F32)
    a_im = a_im.astype(F32)
    step = jnp.exp(log_step.astype(F32))[..., None]
    mag = jnp.exp(a_re * step)
    lam_re = mag * jnp.cos(a_im * step)
    lam_im = mag * jnp.sin(a_im * step)
    den = a_re * a_re + a_im * a_im
    f_re = ((lam_re - 1.0) * a_re + lam_im * a_im) / den
    f_im = (lam_im * a_re - (lam_re - 1.0) * a_im) / den
    b_re = b_re.astype(F32)[None]
    b_im = b_im.astype(F32)[None]
    bb_re = f_re[..., None] * b_re - f_im[..., None] * b_im
    bb_im = f_re[..., None] * b_im + f_im[..., None] * b_re
    ngl = 128 // S5_GROUP
    eye = jnp.eye(ngl, dtype=F32)
    bb = jnp.stack([bb_re, bb_im], axis=1)
    bb = bb.reshape(2, 2, S5_GB, ngl, S5_STATE, S5_GROUP)
    bmat = jnp.einsum('drgaph,ab->gdahrbp', bb, eye).reshape(S5_GB, 2 * 128, 2 * ngl * S5_STATE)
    cc = jnp.stack([c_re.astype(F32), -c_im.astype(F32)], axis=1)
    cc = cc.reshape(2, 2, S5_GB, ngl, S5_GROUP, S5_STATE)
    cmat = jnp.einsum('drgahp,ab->grapdbh', cc, eye).reshape(S5_GB, 2 * ngl * S5_STATE, 2 * 128)
    lam = jnp.stack([lam_re, lam_im], axis=1)
    lam = lam.reshape(2, 2, S5_GB, ngl * S5_STATE).transpose(2, 1, 0, 3)
    lam = jnp.repeat(lam, SUBLANES // 2, axis=2)
    return bmat.astype(BF16), cmat.astype(BF16), lam


def _scan_order(nseq, tblk):
    step = jnp.arange(tblk * SUBLANES) // SUBLANES
    seq = jnp.arange(tblk * SUBLANES) % SUBLANES
    src = jnp.arange(nseq * tblk)[None, :]
    fwd = (seq < nseq)[:, None] & (src == (seq * tblk + step)[:, None])
    bwd = (seq >= nseq)[:, None] & (src == ((seq - nseq) * tblk + (tblk - 1 - step))[:, None])
    return fwd.astype(BF16), bwd.astype(BF16), fwd.T.astype(BF16), bwd.T.astype(BF16)


def _s5(proj, h0, ops, col_block):
    bs, ls, _ = proj.shape
    assert 2 * bs == SUBLANES
    bmat, cmat, lam = ops
    tblk = 64
    nblk = ls // tblk
    rows = tblk * SUBLANES
    perms = _scan_order(bs, tblk)
    st_shape = (S5_GB, 2, SUBLANES, S5_GB * S5_STATE)
    full = lambda shape: pl.BlockSpec(shape, lambda i: (0,) * len(shape))
    y_shape = jax.ShapeDtypeStruct((bs, ls, BRANCH), BF16)
    return pl.pallas_call(
        functools.partial(_s5_kernel, tblk=tblk, nseq=bs),
        out_shape=(y_shape, y_shape, jax.ShapeDtypeStruct(st_shape, F32)),
        grid=(nblk,),
        in_specs=[pl.BlockSpec((bs, tblk, BRANCH), lambda i: (0, i, col_block)),
                  pl.BlockSpec((bs, tblk, BRANCH), lambda i: (0, nblk - 1 - i, col_block))]
                 + [full(p.shape) for p in perms]
                 + [full(bmat.shape), full(cmat.shape), full(lam.shape), full(st_shape)],
        out_specs=(pl.BlockSpec((bs, tblk, BRANCH), lambda i: (0, i, 0)),
                   pl.BlockSpec((bs, tblk, BRANCH), lambda i: (0, nblk - 1 - i, 0)),
                   full(st_shape)),
        scratch_shapes=[pltpu.VMEM((rows, BRANCH), BF16)] * 4
                       + [pltpu.VMEM((2, rows, 2 * S5_GB * S5_STATE), F32), pltpu.VMEM(st_shape, F32)],
        compiler_params=_cparams("arbitrary"),
        name="s5_scan",
    )(proj, proj, *perms, bmat, cmat, lam, h0)


def _mixers_kernel(p_ref, yf_ref, yb_ref, ws_ref, bs_ref, sd_ref, wglu_ref, bglu_ref,
                   pm_ref, ic_ref, pw_ref, ps_ref, o_ref, *, tm):
    for ch in range(tm // CHUNK):
        r = slice(ch * CHUNK, (ch + 1) * CHUNK)
        for h in range(HEADS):
            vn = p_ref[r, BRANCH + h * 128:BRANCH + (h + 1) * 128]
            mixed = _dot(ws_ref[h], vn) + bs_ref[h]
            o_ref[r, h * 128:(h + 1) * 128] = (p_ref[r, h * 128:(h + 1) * 128].astype(F32) * mixed).astype(BF16)
    y = (yf_ref[...].astype(F32) + yb_ref[...].astype(F32)
         + p_ref[:, 2 * BRANCH:3 * BRANCH].astype(F32) * sd_ref[...])
    g = _gelu(y)
    z = _dot(g.astype(BF16), wglu_ref[...]) + bglu_ref[...]
    o_ref[:, BRANCH:2 * BRANCH] = (g * _sigmoid(z)).astype(BF16)
    for w in range(len(POOL_WINDOWS)):
        c0 = 3 * BRANCH + w * POOL_GROUP
        gq = p_ref[:, c0:c0 + POOL_GROUP]
        s = _dot(pm_ref[w], gq)
        yv = s * ic_ref[w] - gq.astype(F32)
        yc = _dot(yv.astype(BF16), pw_ref[w]) * ps_ref[:, w * POOL_GROUP:(w + 1) * POOL_GROUP]
        o0 = 2 * BRANCH + w * POOL_GROUP
        o_ref[:, o0:o0 + POOL_GROUP] = yc.astype(BF16)


def _pool_operators(tm, seg):
    t = jnp.arange(tm)
    base = (t // seg) * seg
    pos = t - base
    mats, invs = [], []
    for k in POOL_WINDOWS:
        lo = base + jnp.clip(pos - k // 2, 0, seg - 1)
        hi = base + jnp.clip(pos + k // 2 - 1, 0, seg - 1)
        mats.append(((t[None, :] >= lo[:, None]) & (t[None, :] <= hi[:, None])).astype(BF16))
        cnt = (hi - lo + 1).astype(F32)
        invs.append(jnp.broadcast_to((1.0 / cnt)[:, None], (tm, POOL_GROUP)))
    return jnp.stack(mats), jnp.stack(invs)


def _mixers(proj, yf, yb, prm, seg):
    t, n = proj.shape
    tm = 256
    assert tm % seg == 0 and tm % CHUNK == 0
    pm, ic = _pool_operators(tm, seg)
    full = lambda a: pl.BlockSpec(a.shape, lambda i: (0,) * a.ndim)
    ws, bs, sd, wglu, bglu, pw, ps = (prm[k] for k in ('ws', 'bs', 'sd', 'wglu', 'bglu', 'pw', 'ps'))
    return pl.pallas_call(
        functools.partial(_mixers_kernel, tm=tm),
        out_shape=jax.ShapeDtypeStruct((t, 3 * BRANCH), BF16),
        grid=(t // tm,),
        in_specs=[pl.BlockSpec((tm, n), lambda i: (i, 0)),
                  pl.BlockSpec((tm, BRANCH), lambda i: (i, 0)),
                  pl.BlockSpec((tm, BRANCH), lambda i: (i, 0)),
                  full(ws), full(bs), full(sd), full(wglu), full(bglu),
                  full(pm), full(ic), full(pw), full(ps)],
        out_specs=pl.BlockSpec((tm, 3 * BRANCH), lambda i: (i, 0)),
        compiler_params=_cparams("parallel"),
        name="mixers",
    )(proj, yf, yb, ws, bs, sd, wglu, bglu, pm, ic, pw, ps)


def _gate_merge_kernel(h_ref, y_ref, wg0_ref, wg1_ref, wg2_ref, bg_ref, wb_ref, o_ref):
    h = h_ref[...]
    acc = None
    for k, wg_ref in enumerate((wg0_ref, wg1_ref, wg2_ref)):
        gate = _sigmoid(_dot(h, wg_ref[...]) + bg_ref[k])
        term = gate * _dot(y_ref[:, k * BRANCH:(k + 1) * BRANCH], wb_ref[k])
        acc = term if acc is None else acc + term
    o_ref[...] = acc.astype(BF16)


def _gate_merge(h, yabc, w_gate, bg, wb, layer):
    t, d = h.shape
    nj, tn = w_gate.shape[2], w_gate.shape[4]
    tm = min(512, t)
    gate_cols = lambda k: pl.BlockSpec((None, None, None, d, tn), lambda j, i: (layer, k, j, 0, 0))
    return pl.pallas_call(
        _gate_merge_kernel,
        out_shape=jax.ShapeDtypeStruct((t, d), BF16),
        grid=(nj, t // tm),
        in_specs=[pl.BlockSpec((tm, d), lambda j, i: (i, 0)),
                  pl.BlockSpec((tm, 3 * BRANCH), lambda j, i: (i, 0)),
                  gate_cols(0), gate_cols(1), gate_cols(2),
                  pl.BlockSpec((3, 1, tn), lambda j, i: (0, 0, j)),
                  pl.BlockSpec((None, None, 3, BRANCH, tn), lambda j, i: (layer, j, 0, 0, 0))],
        out_specs=pl.BlockSpec((tm, tn), lambda j, i: (i, j)),
        compiler_params=_cparams("parallel", "parallel"),
        name="gate_merge",
    )(h, yabc, w_gate, w_gate, w_gate, bg, wb)


def _out_ln_kernel(m_ref, w_ref, x_ref, g_ref, lg_ref, lb_ref, sc_ref, sh_ref, wr_ref,
                   x1_ref, h2_ref, aff_ref, pre_ref, *, nj, tn, alpha, n_experts):
    j = pl.program_id(1)
    pre_ref[j] = alpha * x_ref[...] + g_ref[0] * _dot(m_ref[...], w_ref[...])

    @pl.when(j == nj - 1)
    def _():
        d = nj * tn
        tot = None
        for jj in range(nj):
            s = jnp.sum(pre_ref[jj], axis=-1, keepdims=True)
            tot = s if tot is None else tot + s
        mu = tot / d
        tot = None
        for jj in range(nj):
            dl = pre_ref[jj] - mu
            s = jnp.sum(dl * dl, axis=-1, keepdims=True)
            tot = s if tot is None else tot + s
        rstd = lax.rsqrt(tot / d + LN_EPS)
        for jj in range(nj):
            cs = slice(jj * tn, (jj + 1) * tn)
            y = (pre_ref[jj] - mu) * rstd * lg_ref[:, cs] + lb_ref[:, cs]
            x1_ref[:, cs] = y
            h2_ref[:, cs] = (y * (1.0 + sc_ref[0, :, cs]) + sh_ref[0, :, cs]).astype(BF16)
        logits = _dot(h2_ref[...], wr_ref[...])
        lane = lax.broadcasted_iota(jnp.int32, logits.shape, 1)
        logits = jnp.where(lane < n_experts, logits, -jnp.inf)
        e = jnp.exp(logits - jnp.max(logits, axis=-1, keepdims=True))
        aff_ref[...] = e / jnp.sum(e, axis=-1, keepdims=True)


def _out_ln(merged, w_out, x, gate, ln_g, ln_b, sc2, sh2, w_router, layer, seq_len, alpha, n_experts):
    t, d = x.shape
    nj, tn = w_out.shape[1], w_out.shape[3]
    tm = min(512, seq_len)
    per = seq_len // tm
    row = pl.BlockSpec((1, d), lambda i, j: (0, 0))
    vec = pl.BlockSpec((1, 1, d), lambda i, j: (i // per, 0, 0))
    return pl.pallas_call(
        functools.partial(_out_ln_kernel, nj=nj, tn=tn, alpha=alpha, n_experts=n_experts),
        out_shape=(jax.ShapeDtypeStruct((t, d), F32),
                   jax.ShapeDtypeStruct((t, d), BF16),
                   jax.ShapeDtypeStruct((t, ROUTER_LANES), F32)),
        grid=(t // tm, nj),
        in_specs=[pl.BlockSpec((tm, d), lambda i, j: (i, 0)),
                  pl.BlockSpec((None, None, d, tn), lambda i, j: (layer, j, 0, 0)),
                  pl.BlockSpec((tm, tn), lambda i, j: (i, j)),
                  pl.BlockSpec((1, 1, tn), lambda i, j: (i // per, 0, j)),
                  row, row, vec, vec,
                  pl.BlockSpec((d, ROUTER_LANES), lambda i, j: (0, 0))],
        out_specs=(pl.BlockSpec((tm, d), lambda i, j: (i, 0)),
                   pl.BlockSpec((tm, d), lambda i, j: (i, 0)),
                   pl.BlockSpec((tm, ROUTER_LANES), lambda i, j: (i, 0))),
        scratch_shapes=[pltpu.VMEM((nj, tm, tn), F32)],
        compiler_params=_cparams("parallel", "arbitrary"),
        name="out_ln",
    )(merged, w_out, x, gate, ln_g, ln_b, sc2, sh2, w_router)


def _ffn_kernel(x_ref, w1_ref, w3_ref, w2_ref, tw_ref, tag_ref, o_ref):
    d = x_ref.shape[2]
    x = x_ref[0]
    h1 = _dot(x, w1_ref[0])
    hid = (h1 * _sigmoid(h1)) * _dot(x, w3_ref[0])
    o_ref[0, :, :d] = (_dot(hid.astype(BF16), w2_ref[0]) * tw_ref[0]).astype(BF16)
    o_ref[0, :, d:] = tag_ref[0]


def _expert_ffn(xs, w1, w3, w2, tw, tag, layer):
    e, m, d = xs.shape
    ff = w1.shape[3]
    tm = min(512, m)
    return pl.pallas_call(
        _ffn_kernel,
        out_shape=jax.ShapeDtypeStruct((e, m, d + TAG_LANES), BF16),
        grid=(e, m // tm),
        in_specs=[pl.BlockSpec((1, tm, d), lambda ei, i: (ei, i, 0)),
                  pl.BlockSpec((None, 1, d, ff), lambda ei, i: (layer, ei, 0, 0)),
                  pl.BlockSpec((None, 1, d, ff), lambda ei, i: (layer, ei, 0, 0)),
                  pl.BlockSpec((None, 1, ff, d), lambda ei, i: (layer, ei, 0, 0)),
                  pl.BlockSpec((1, tm, 1), lambda ei, i: (ei, i, 0)),
                  pl.BlockSpec((1, tm, TAG_LANES), lambda ei, i: (ei, i, 0))],
        out_specs=pl.BlockSpec((1, tm, d + TAG_LANES), lambda ei, i: (ei, i, 0)),
        compiler_params=_cparams("parallel", "parallel"),
        name="expert_ffn",
    )(xs, w1, w3, w2, tw, tag)


def _tag_tokens(tags):
    lane = lax.broadcasted_iota(jnp.int32, (SUBLANES, TAG_LANES), 1)
    coef = jnp.where(lane == 0, float(TAG_BASE), jnp.where(lane == 1, 1.0, 0.0)).astype(BF16)
    return lax.dot_general(coef, tags, (((1,), (1,)), ((), ())), preferred_element_type=F32)[0:1]


def _select_rows(tok, valid, first_token, tm):
    ids = (first_token + lax.broadcasted_iota(jnp.int32, (tm, tok.shape[1]), 0)).astype(F32)
    return jnp.where(jnp.where(valid, tok, -1.0) == ids, 1.0, 0.0).astype(BF16)


def _combine_ln2_kernel(s0_ref, cnt_ref, x_ref, oe_ref, g_ref, lg_ref, lb_ref, sc_ref, sh_ref,
                        x2_ref, h_ref, win_ref, xw_ref, acc_ref, sem, xsem,
                        *, alpha, n_exp, win, tm, total_rows):
    i = pl.program_id(0)
    d = x_ref.shape[1]
    slot = i % 2
    last_start = total_rows - win
    cover = win - BF16_ROW_TILE
    first_token = i * tm

    def window_start(first_row):
        start = (jnp.minimum(first_row, last_start) // BF16_ROW_TILE) * BF16_ROW_TILE
        return pl.multiple_of(start, BF16_ROW_TILE)

    def window_copy(start, buf, e):
        return pltpu.make_async_copy(oe_ref.at[pl.ds(start, win)],
                                     win_ref.at[buf, pl.ds(e * win, win)], sem.at[buf, e])

    def fetch(tile, buf):
        for e in range(n_exp):
            window_copy(window_start(s0_ref[tile * n_exp + e]), buf, e).start()

    @pl.when(i == 0)
    def _():
        fetch(0, 0)

    @pl.when(i + 1 < pl.num_programs(0))
    def _():
        fetch(i + 1, 1 - slot)

    for e in range(n_exp):
        window_copy(0, slot, e).wait()

    wl = lax.broadcasted_iota(jnp.int32, (1, n_exp * win), 1)
    row = jnp.zeros_like(wl)
    lo = jnp.zeros_like(wl)
    hi = jnp.zeros_like(wl)
    for e in range(n_exp):
        s0 = s0_ref[i * n_exp + e]
        cnt = cnt_ref[i * n_exp + e]
        in_e = (wl >= e * win) & (wl < (e + 1) * win)
        row = jnp.where(in_e, window_start(s0) + (wl - e * win), row)
        lo = jnp.where(in_e, s0, lo)
        hi = jnp.where(in_e, s0 + jnp.minimum(cnt, cover), hi)
    valid = (row >= lo) & (row < hi)
    tok = _tag_tokens(win_ref[slot, :, d:])
    acc_ref[...] = _dot(_select_rows(tok, valid, first_token, tm), win_ref[slot, :, :d])

    for e in range(n_exp):
        s0 = s0_ref[i * n_exp + e]
        cnt = cnt_ref[i * n_exp + e]

        def extra(k, carry, s0=s0, cnt=cnt):
            lo_k = s0 + k * cover
            start = window_start(lo_k)
            cp = pltpu.make_async_copy(oe_ref.at[pl.ds(start, win)], xw_ref, xsem.at[0])
            cp.start()
            cp.wait()
            row_k = start + lax.broadcasted_iota(jnp.int32, (1, win), 1)
            ok = (row_k >= lo_k) & (row_k < jnp.minimum(lo_k + cover, s0 + cnt))
            sel = _select_rows(_tag_tokens(xw_ref[:, d:]), ok, first_token, tm)
            acc_ref[...] += _dot(sel, xw_ref[:, :d])
            return carry

        lax.fori_loop(1, (cnt + cover - 1) // cover, extra, 0)

    pre = alpha * x_ref[...] + g_ref[0] * acc_ref[...]
    dl = pre - jnp.mean(pre, axis=-1, keepdims=True)
    var = jnp.mean(dl * dl, axis=-1, keepdims=True)
    y = dl * lax.rsqrt(var + LN_EPS) * lg_ref[...] + lb_ref[...]
    x2_ref[...] = y
    h_ref[...] = (y * (1.0 + sc_ref[0]) + sh_ref[0]).astype(BF16)


def _combine_ln2(x, oe, s0, cnt, gate, ln_g, ln_b, sc, sh, seq_len, tm, alpha, n_exp):
    t, d = x.shape
    total_rows = oe.shape[0]
    win = COMBINE_WINDOW
    assert total_rows >= win and total_rows % BF16_ROW_TILE == 0 and t <= TAG_BASE * 256
    per = seq_len // tm
    tile = pl.BlockSpec((tm, d), lambda i, *_: (i, 0))
    row = pl.BlockSpec((1, d), lambda i, *_: (0, 0))
    vec = pl.BlockSpec((1, 1, d), lambda i, *_: (i // per, 0, 0))
    return pl.pallas_call(
        functools.partial(_combine_ln2_kernel, alpha=alpha, n_exp=n_exp, win=win, tm=tm,
                          total_rows=total_rows),
        out_shape=(jax.ShapeDtypeStruct((t, d), F32), jax.ShapeDtypeStruct((t, d), BF16)),
        grid_spec=pltpu.PrefetchScalarGridSpec(
            num_scalar_prefetch=2,
            grid=(t // tm,),
            in_specs=[tile, pl.BlockSpec(memory_space=pl.ANY), vec, row, row, vec, vec],
            out_specs=(tile, tile),
            scratch_shapes=[pltpu.VMEM((2, n_exp * win, d + TAG_LANES), BF16),
                            pltpu.VMEM((win, d + TAG_LANES), BF16),
                            pltpu.VMEM((tm, d), F32),
                            pltpu.SemaphoreType.DMA((2, n_exp)),
                            pltpu.SemaphoreType.DMA((1,))]),
        compiler_params=_cparams("arbitrary"),
        name="combine_ln2",
    )(s0, cnt, x, oe, gate, ln_g, ln_b, sc, sh)


def _mixer_inputs(h, lp, bs, ls, h0):
    proj = _proj(h, lp['w_in'], lp['layer'])
    yf, yb, hfin = _s5(proj.reshape(bs, ls, -1), h0, lp['s5_ops'], col_block=2)
    return proj, yf.reshape(bs * ls, BRANCH), yb.reshape(bs * ls, BRANCH), hfin


def _expert_choice(h2, aff, lp, bs, ls, tm):
    n_exp = lp['w1'].shape[1]
    d = h2.shape[1]
    cap = CAPACITY_FACTOR * ls // n_exp
    m = bs * cap
    a = aff[:, :n_exp].reshape(bs, ls, n_exp)
    top_w, top_idx = lax.top_k(jnp.swapaxes(a, 1, 2), cap)
    top_idx, top_w = lax.sort((top_idx, top_w), dimension=2, num_keys=1)
    gidx = top_idx + (jnp.arange(bs, dtype=top_idx.dtype) * ls)[:, None, None]
    gidx = gidx.transpose(1, 0, 2).reshape(-1)
    xs = h2.at[gidx].get(mode='promise_in_bounds').reshape(n_exp, m, d)
    tw = top_w.transpose(1, 0, 2).reshape(n_exp, m, 1)
    tag = jnp.stack([gidx // TAG_BASE, gidx % TAG_BASE], axis=-1).astype(BF16)
    tag = jnp.pad(tag, ((0, 0), (0, TAG_LANES - 2))).reshape(n_exp, m, TAG_LANES)
    oe = _expert_ffn(xs, lp['w1'], lp['w3'], lp['w2'], tw, tag, lp['layer'])
    tile_start = jnp.arange(ls // tm, dtype=top_idx.dtype) * tm
    below = jnp.sum(top_idx[:, :, None, :] < tile_start[None, None, :, None], axis=-1)
    upto = jnp.concatenate([below[:, :, 1:], jnp.full((bs, n_exp, 1), cap, below.dtype)], axis=2)
    base = (jnp.arange(n_exp) * m)[None, :, None] + (jnp.arange(bs) * cap)[:, None, None]
    s0 = (base + below).transpose(0, 2, 1).reshape(-1).astype(jnp.int32)
    cnt = (upto - below).transpose(0, 2, 1).reshape(-1).astype(jnp.int32)
    return oe.reshape(n_exp * m, d + TAG_LANES), s0, cnt


def _layer(x, h, mod, nxt, lp, bs, ls, seg, h0, alpha):
    proj, yf, yb, hfin = _mixer_inputs(h, lp, bs, ls, h0)
    yabc = _mixers(proj, yf, yb, lp, seg)
    merged = _gate_merge(h, yabc, lp['wg'], lp['bg'], lp['wb'], lp['layer'])
    n_exp = lp['w1'].shape[1]
    x1, h2, aff = _out_ln(merged, lp['w_out'], x, mod[2], lp['ln1_g'], lp['ln1_b'], mod[4], mod[3],
                          lp['w_router'], lp['layer'], ls, alpha, n_exp)
    tm = min(256, ls)
    oe, s0, cnt = _expert_choice(h2, aff, lp, bs, ls, tm)
    x2, hn = _combine_ln2(x1, oe, s0, cnt, mod[5], lp['ln2_g'], lp['ln2_b'], nxt[1], nxt[0],
                          ls, tm, alpha, n_exp)
    return x2, hn, hfin


def kernel(x, c, ctx, c_ctx, ada_w_down, ada_w_up, ada_b, w_in, gmlp_w_s, gmlp_b_s, s5_a_re, s5_a_im, s5_log_step, s5_b_re, s5_b_im, s5_c_re, s5_c_im, s5_d, s5_w_glu, s5_b_glu, pool_w, pool_scale, w_gate, b_gate, w_branch, w_out, ln1_g, ln1_b, w_router, w1, w3, w2, ln2_g, ln2_b):
    bs, n, d = x.shape
    lc = ctx.shape[1]
    depth = w_in.shape[0]
    n_exp = w_router.shape[2]
    alpha = (2.0 * depth) ** 0.25
    assert bs + 1 <= SUBLANES and n % GRID_W == 0

    cond = jnp.zeros((SUBLANES, d), F32).at[:bs].set(c).at[bs].set(c_ctx)
    mods = _ada(cond, ada_w_down, ada_w_up, ada_b).reshape(depth, SUBLANES, N_MOD, d)

    def lat_mod(l):
        return mods[l, :bs].transpose(1, 0, 2)[:, :, None, :]

    def ctx_mod(l):
        return jnp.broadcast_to(mods[l, bs][:, None, None, :], (N_MOD, bs, 1, d))

    tn = COLUMN_BLOCK
    nj = d // tn
    wg_b = _block_columns(w_gate, tn).reshape(depth, 3, nj, d, tn)
    wb_b = _block_columns(w_branch.reshape(depth, 3 * BRANCH, d), tn).reshape(depth, nj, 3, BRANCH, tn)
    wo_b = _block_columns(w_out, tn)
    w1_b, w3_b, w2_b = w1.astype(BF16), w3.astype(BF16), w2.astype(BF16)

    def layer_params(l):
        return dict(
            layer=l, w_in=w_in,
            s5_ops=_s5_operators(s5_a_re[l], s5_a_im[l], s5_log_step[l], s5_b_re[l], s5_b_im[l],
                                 s5_c_re[l], s5_c_im[l]),
            ws=gmlp_w_s[l].astype(BF16),
            bs=jnp.broadcast_to(gmlp_b_s[l][:, :, None], (HEADS, CHUNK, 128)).astype(F32),
            sd=s5_d[l].reshape(1, BRANCH), wglu=s5_w_glu[l].astype(BF16), bglu=s5_b_glu[l].reshape(1, BRANCH),
            pw=pool_w[l].astype(BF16), ps=pool_scale[l].reshape(1, BRANCH),
            wg=wg_b, bg=b_gate[l].reshape(3, 1, d), wb=wb_b,
            w_out=wo_b, ln1_g=ln1_g[l].reshape(1, d), ln1_b=ln1_b[l].reshape(1, d),
            w_router=jnp.zeros((d, ROUTER_LANES), BF16).at[:, :n_exp].set(w_router[l].astype(BF16)),
            w1=w1_b, w3=w3_b, w2=w2_b,
            ln2_g=ln2_g[l].reshape(1, d), ln2_b=ln2_b[l].reshape(1, d))

    xl = x.reshape(bs * n, d)
    xc = ctx.reshape(bs * lc, d)
    m0, c0 = lat_mod(0), ctx_mod(0)
    hl = _modulate(xl, m0[0], m0[1], n)
    hc = _modulate(xc, c0[0], c0[1], lc)
    zero_state = jnp.zeros((S5_GB, 2, SUBLANES, S5_GB * S5_STATE), F32)
    for l in range(depth):
        lp = layer_params(l)
        nl = min(l + 1, depth - 1)
        if l < depth - 1:
            xc, hc, ctx_final = _layer(xc, hc, ctx_mod(l), ctx_mod(nl), lp, bs, lc, lc, zero_state, alpha)
        else:
            ctx_final = _mixer_inputs(hc, lp, bs, lc, zero_state)[-1]
        xl, hl, _ = _layer(xl, hl, lat_mod(l), lat_mod(nl), lp, bs, n, GRID_W, ctx_final, alpha)
    return xl.reshape(bs, n, d)
```

```python
import functools
import math

import jax
import jax.numpy as jnp
from jax import lax
from jax.experimental import pallas as pl
from jax.experimental.pallas import tpu as pltpu

F32 = jnp.float32
BF16 = jnp.bfloat16

LN_EPS = 1e-6
GRID_W = 64
CHUNK = 128
BRANCH = 1024
HEADS = 8
S5_GROUP = 16
S5_STATE = 64
S5_GB = 8
POOL_WINDOWS = (2, 4, 8, 16)
POOL_GROUP = BRANCH // len(POOL_WINDOWS)
N_MOD = 6
ROUTER_LANES = 128
CAPACITY_FACTOR = 2
TAG_LANES = 128
TAG_BASE = 128
COMBINE_WINDOW = 64
COLUMN_BLOCK = 512
W_RING = 3

V7X_VMEM_LIMIT = 60 * 1024 * 1024
SUBLANES = 8
BF16_ROW_TILE = 16


def _cparams(*sem):
    return pltpu.CompilerParams(dimension_semantics=sem, vmem_limit_bytes=V7X_VMEM_LIMIT)


def _sigmoid(x):
    return 1.0 / (1.0 + jnp.exp(-x))


def _gelu(x):
    c = math.sqrt(2.0 / math.pi)
    return x * (0.5 * (1.0 + jnp.tanh(c * (x + 0.044715 * (x * x * x)))))


def _dot(a, b):
    return jnp.dot(a, b, preferred_element_type=F32)


def _ada_down_kernel(c_ref, w_ref, o_ref):
    @pl.when(pl.program_id(1) == 0)
    def _():
        o_ref[...] = jnp.zeros_like(o_ref)

    c = c_ref[...]
    o_ref[0] += _dot((c * _sigmoid(c)).astype(BF16), w_ref[0].astype(BF16))


def _ada_up_kernel(r_ref, w_ref, b_ref, o_ref):
    o_ref[0] = _dot(r_ref[0].astype(BF16), w_ref[0].astype(BF16)) + b_ref[0]


def _ada(cond, w_down, w_up, b_up):
    depth, d, rank = w_down.shape
    nmod = w_up.shape[2]
    tk, tn = 1024, 2048
    r = pl.pallas_call(
        _ada_down_kernel,
        out_shape=jax.ShapeDtypeStruct((depth, SUBLANES, rank), F32),
        grid=(depth, d // tk),
        in_specs=[pl.BlockSpec((SUBLANES, tk), lambda l, k: (0, k)),
                  pl.BlockSpec((1, tk, rank), lambda l, k: (l, k, 0))],
        out_specs=pl.BlockSpec((1, SUBLANES, rank), lambda l, k: (l, 0, 0)),
        compiler_params=_cparams("parallel", "arbitrary"),
        name="ada_down",
    )(cond, w_down)
    return pl.pallas_call(
        _ada_up_kernel,
        out_shape=jax.ShapeDtypeStruct((depth, SUBLANES, nmod), F32),
        grid=(depth, nmod // tn),
        in_specs=[pl.BlockSpec((1, SUBLANES, rank), lambda l, n: (l, 0, 0)),
                  pl.BlockSpec((1, rank, tn), lambda l, n: (l, 0, n)),
                  pl.BlockSpec((1, 1, tn), lambda l, n: (l, 0, n))],
        out_specs=pl.BlockSpec((1, SUBLANES, tn), lambda l, n: (l, 0, n)),
        compiler_params=_cparams("parallel", "parallel"),
        name="ada_up",
    )(r, w_up, b_up.reshape(depth, 1, nmod))


def _block_columns_kernel(w_ref, o_ref, *, tn):
    for j in range(o_ref.shape[0]):
        o_ref[j] = w_ref[:, j * tn:(j + 1) * tn].astype(BF16)


def _block_columns(w, tn):
    depth, k, n = w.shape
    rk = 256
    return pl.pallas_call(
        functools.partial(_block_columns_kernel, tn=tn),
        out_shape=jax.ShapeDtypeStruct((depth, n // tn, k, tn), BF16),
        grid=(depth, k // rk),
        in_specs=[pl.BlockSpec((None, rk, n), lambda l, r: (l, r, 0))],
        out_specs=pl.BlockSpec((None, n // tn, rk, tn), lambda l, r: (l, 0, r, 0)),
        compiler_params=_cparams("parallel", "parallel"),
        name="block_columns",
    )(w)


def _modulate_kernel(x_ref, sh_ref, sc_ref, o_ref):
    o_ref[...] = (x_ref[...] * (1.0 + sc_ref[0]) + sh_ref[0]).astype(BF16)


def _modulate(x, sh, sc, seq_len):
    t, d = x.shape
    tm = min(512, seq_len)
    per = seq_len // tm
    vec = pl.BlockSpec((1, 1, d), lambda i: (i // per, 0, 0))
    return pl.pallas_call(
        _modulate_kernel,
        out_shape=jax.ShapeDtypeStruct((t, d), BF16),
        grid=(t // tm,),
        in_specs=[pl.BlockSpec((tm, d), lambda i: (i, 0)), vec, vec],
        out_specs=pl.BlockSpec((tm, d), lambda i: (i, 0)),
        compiler_params=_cparams("parallel"),
        name="modulate",
    )(x, sh, sc)


def _proj_kernel(a_ref, w_ref, o_ref, wb_ref):
    j = pl.program_id(0)

    @pl.when(pl.program_id(1) == 0)
    def _():
        wb_ref[...] = w_ref[...].astype(BF16)

    acc = _dot(a_ref[...], wb_ref[...])

    @pl.when(j == 0)
    def _():
        o_ref[...] = _gelu(acc).astype(BF16)

    @pl.when(j == 1)
    def _():
        g = _gelu(acc)
        dlt = g - jnp.mean(g, axis=-1, keepdims=True)
        var = jnp.mean(dlt * dlt, axis=-1, keepdims=True)
        o_ref[...] = (dlt * lax.rsqrt(var + LN_EPS)).astype(BF16)

    @pl.when(j >= 2)
    def _():
        o_ref[...] = acc.astype(BF16)


def _proj(h, w_in, layer):
    t, d = h.shape
    n = w_in.shape[2]
    tm = min(512, t)
    return pl.pallas_call(
        _proj_kernel,
        out_shape=jax.ShapeDtypeStruct((t, n), BF16),
        grid=(n // BRANCH, t // tm),
        in_specs=[pl.BlockSpec((tm, d), lambda j, i: (i, 0)),
                  pl.BlockSpec((None, d, BRANCH), lambda j, i: (layer, 0, j))],
        out_specs=pl.BlockSpec((tm, BRANCH), lambda j, i: (i, j)),
        scratch_shapes=[pltpu.VMEM((d, BRANCH), BF16)],
        compiler_params=_cparams("parallel", "arbitrary"),
        name="proj",
    )(h, w_in)


def _s5_kernel(uf_ref, ub_ref, pf_ref, pb_ref, pft_ref, pbt_ref, bm_ref, cm_ref, lam_ref, h0_ref,
               yf_ref, yb_ref, hfin_ref, uf_s, ub_s, of_s, ob_s, bu_ref, st_ref, *, tblk, nseq):
    i = pl.program_id(0)
    half = S5_GB * S5_STATE
    src_rows = nseq * tblk

    @pl.when(i == 0)
    def _():
        st_ref[...] = h0_ref[...]

    uf_s[...] = _dot(pf_ref[...], uf_ref[...].reshape(src_rows, BRANCH)).astype(BF16)
    ub_s[...] = _dot(pb_ref[...], ub_ref[...].reshape(src_rows, BRANCH)).astype(BF16)

    for gb in range(S5_GB):
        cols = slice(gb * 128, (gb + 1) * 128)
        bu = bu_ref.at[gb % 2]
        bu[...] = _dot(jnp.concatenate([uf_s[:, cols], ub_s[:, cols]], axis=1), bm_ref[gb])
        lr = lam_ref[gb, 0]
        li = lam_ref[gb, 1]

        def step(t, carry, bu=bu, lr=lr, li=li):
            hr, hi = carry
            r0 = pl.multiple_of(t * SUBLANES, SUBLANES)
            nr = lr * hr - li * hi + bu[pl.ds(r0, SUBLANES), 0:half]
            ni = lr * hi + li * hr + bu[pl.ds(r0, SUBLANES), half:2 * half]
            bu[pl.ds(r0, SUBLANES), 0:half] = nr
            bu[pl.ds(r0, SUBLANES), half:2 * half] = ni
            return nr, ni

        hr, hi = lax.fori_loop(0, tblk, step, (st_ref[gb, 0], st_ref[gb, 1]), unroll=True)
        st_ref[gb, 0] = hr
        st_ref[gb, 1] = hi
        yy = _dot(bu[...].astype(BF16), cm_ref[gb]).astype(BF16)
        of_s[:, cols] = yy[:, :128]
        ob_s[:, cols] = yy[:, 128:]

    yf_ref[...] = _dot(pft_ref[...], of_s[...]).astype(BF16).reshape(nseq, tblk, BRANCH)
    yb_ref[...] = _dot(pbt_ref[...], ob_s[...]).astype(BF16).reshape(nseq, tblk, BRANCH)

    @pl.when(i == pl.num_programs(0) - 1)
    def _():
        hfin_ref[...] = st_ref[...]


def _s5_operators(a_re, a_im, log_step, b_re, b_im, c_re, c_im):
    a_re = a_re.astype(F32)
    a_im = a_im.astype(F32)
    step = jnp.exp(log_step.astype(F32))[..., None]
    mag = jnp.exp(a_re * step)
    lam_re = mag * jnp.cos(a_im * step)
    lam_im = mag * jnp.sin(a_im * step)
    den = a_re * a_re + a_im * a_im
    f_re = ((lam_re - 1.0) * a_re + lam_im * a_im) / den
    f_im = (lam_im * a_re - (lam_re - 1.0) * a_im) / den
    b_re = b_re.astype(F32)[None]
    b_im = b_im.astype(F32)[None]
    bb_re = f_re[..., None] * b_re - f_im[..., None] * b_im
    bb_im = f_re[..., None] * b_im + f_im[..., None] * b_re
    ngl = 128 // S5_GROUP
    eye = jnp.eye(ngl, dtype=F32)
    bb = jnp.stack([bb_re, bb_im], axis=1)
    bb = bb.reshape(2, 2, S5_GB, ngl, S5_STATE, S5_GROUP)
    bmat = jnp.einsum('drgaph,ab->gdahrbp', bb, eye).reshape(S5_GB, 2 * 128, 2 * ngl * S5_STATE)
    cc = jnp.stack([c_re.astype(F32), -c_im.astype(F32)], axis=1)
    cc = cc.reshape(2, 2, S5_GB, ngl, S5_GROUP, S5_STATE)
    cmat = jnp.einsum('drgahp,ab->grapdbh', cc, eye).reshape(S5_GB, 2 * ngl * S5_STATE, 2 * 128)
    lam = jnp.stack([lam_re, lam_im], axis=1)
    lam = lam.reshape(2, 2, S5_GB, ngl * S5_STATE).transpose(2, 1, 0, 3)
    lam = jnp.repeat(lam, SUBLANES // 2, axis=2)
    return bmat.astype(BF16), cmat.astype(BF16), lam


def _scan_order(nseq, tblk):
    step = jnp.arange(tblk * SUBLANES) // SUBLANES
    seq = jnp.arange(tblk * SUBLANES) % SUBLANES
    src = jnp.arange(nseq * tblk)[None, :]
    fwd = (seq < nseq)[:, None] & (src == (seq * tblk + step)[:, None])
    bwd = (seq >= nseq)[:, None] & (src == ((seq - nseq) * tblk + (tblk - 1 - step))[:, None])
    return fwd.astype(BF16), bwd.astype(BF16), fwd.T.astype(BF16), bwd.T.astype(BF16)


def _s5(proj, h0, ops, col_block):
    bs, ls, _ = proj.shape
    assert 2 * bs == SUBLANES
    bmat, cmat, lam = ops
    tblk = 64
    nblk = ls // tblk
    rows = tblk * SUBLANES
    perms = _scan_order(bs, tblk)
    st_shape = (S5_GB, 2, SUBLANES, S5_GB * S5_STATE)
    full = lambda shape: pl.BlockSpec(shape, lambda i: (0,) * len(shape))
    y_shape = jax.ShapeDtypeStruct((bs, ls, BRANCH), BF16)
    return pl.pallas_call(
        functools.partial(_s5_kernel, tblk=tblk, nseq=bs),
        out_shape=(y_shape, y_shape, jax.ShapeDtypeStruct(st_shape, F32)),
        grid=(nblk,),
        in_specs=[pl.BlockSpec((bs, tblk, BRANCH), lambda i: (0, i, col_block)),
                  pl.BlockSpec((bs, tblk, BRANCH), lambda i: (0, nblk - 1 - i, col_block))]
                 + [full(p.shape) for p in perms]
                 + [full(bmat.shape), full(cmat.shape), full(lam.shape), full(st_shape)],
        out_specs=(pl.BlockSpec((bs, tblk, BRANCH), lambda i: (0, i, 0)),
                   pl.BlockSpec((bs, tblk, BRANCH), lambda i: (0, nblk - 1 - i, 0)),
                   full(st_shape)),
        scratch_shapes=[pltpu.VMEM((rows, BRANCH), BF16)] * 4
                       + [pltpu.VMEM((2, rows, 2 * S5_GB * S5_STATE), F32), pltpu.VMEM(st_shape, F32)],
        compiler_params=_cparams("arbitrary"),
        name="s5_scan",
    )(proj, proj, *perms, bmat, cmat, lam, h0)


def _mixers_kernel(p_ref, yf_ref, yb_ref, ws_ref, bs_ref, sd_ref, wglu_ref, bglu_ref,
                   pm_ref, ic_ref, pw_ref, ps_ref, o_ref, *, tm):
    for ch in range(tm // CHUNK):
        r = slice(ch * CHUNK, (ch + 1) * CHUNK)
        for h in range(HEADS):
            vn = p_ref[r, BRANCH + h * 128:BRANCH + (h + 1) * 128]
            mixed = _dot(ws_ref[h], vn) + bs_ref[h]
            o_ref[r, h * 128:(h + 1) * 128] = (p_ref[r, h * 128:(h + 1) * 128].astype(F32) * mixed).astype(BF16)
    y = (yf_ref[...].astype(F32) + yb_ref[...].astype(F32)
         + p_ref[:, 2 * BRANCH:3 * BRANCH].astype(F32) * sd_ref[...])
    g = _gelu(y)
    z = _dot(g.astype(BF16), wglu_ref[...]) + bglu_ref[...]
    o_ref[:, BRANCH:2 * BRANCH] = (g * _sigmoid(z)).astype(BF16)
    for w in range(len(POOL_WINDOWS)):
        c0 = 3 * BRANCH + w * POOL_GROUP
        gq = p_ref[:, c0:c0 + POOL_GROUP]
        s = _dot(pm_ref[w], gq)
        yv = s * ic_ref[w] - gq.astype(F32)
        yc = _dot(yv.astype(BF16), pw_ref[w]) * ps_ref[:, w * POOL_GROUP:(w + 1) * POOL_GROUP]
        o0 = 2 * BRANCH + w * POOL_GROUP
        o_ref[:, o0:o0 + POOL_GROUP] = yc.astype(BF16)


def _pool_operators(tm, seg):
    t = jnp.arange(tm)
    base = (t // seg) * seg
    pos = t - base
    mats, invs = [], []
    for k in POOL_WINDOWS:
        lo = base + jnp.clip(pos - k // 2, 0, seg - 1)
        hi = base + jnp.clip(pos + k // 2 - 1, 0, seg - 1)
        mats.append(((t[None, :] >= lo[:, None]) & (t[None, :] <= hi[:, None])).astype(BF16))
        cnt = (hi - lo + 1).astype(F32)
        invs.append(jnp.broadcast_to((1.0 / cnt)[:, None], (tm, POOL_GROUP)))
    return jnp.stack(mats), jnp.stack(invs)


def _mixers(proj, yf, yb, prm, seg):
    t, n = proj.shape
    tm = 256
    assert tm % seg == 0 and tm % CHUNK == 0
    pm, ic = _pool_operators(tm, seg)
    full = lambda a: pl.BlockSpec(a.shape, lambda i: (0,) * a.ndim)
    ws, bs, sd, wglu, bglu, pw, ps = (prm[k] for k in ('ws', 'bs', 'sd', 'wglu', 'bglu', 'pw', 'ps'))
    return pl.pallas_call(
        functools.partial(_mixers_kernel, tm=tm),
        out_shape=jax.ShapeDtypeStruct((t, 3 * BRANCH), BF16),
        grid=(t // tm,),
        in_specs=[pl.BlockSpec((tm, n), lambda i: (i, 0)),
                  pl.BlockSpec((tm, BRANCH), lambda i: (i, 0)),
                  pl.BlockSpec((tm, BRANCH), lambda i: (i, 0)),
                  full(ws), full(bs), full(sd), full(wglu), full(bglu),
                  full(pm), full(ic), full(pw), full(ps)],
        out_specs=pl.BlockSpec((tm, 3 * BRANCH), lambda i: (i, 0)),
        compiler_params=_cparams("parallel"),
        name="mixers",
    )(proj, yf, yb, ws, bs, sd, wglu, bglu, pm, ic, pw, ps)


def _gate_merge_kernel(h_ref, y_ref, wg0_ref, wg1_ref, wg2_ref, bg_ref, wb_ref, o_ref):
    h = h_ref[...]
    acc = None
    for k, wg_ref in enumerate((wg0_ref, wg1_ref, wg2_ref)):
        gate = _sigmoid(_dot(h, wg_ref[...]) + bg_ref[k])
        term = gate * _dot(y_ref[:, k * BRANCH:(k + 1) * BRANCH], wb_ref[k])
        acc = term if acc is None else acc + term
    o_ref[...] = acc.astype(BF16)


def _gate_merge(h, yabc, w_gate, bg, wb, layer):
    t, d = h.shape
    nj, tn = w_gate.shape[2], w_gate.shape[4]
    tm = min(512, t)
    gate_cols = lambda k: pl.BlockSpec((None, None, None, d, tn), lambda j, i: (layer, k, j, 0, 0))
    return pl.pallas_call(
        _gate_merge_kernel,
        out_shape=jax.ShapeDtypeStruct((t, d), BF16),
        grid=(nj, t // tm),
        in_specs=[pl.BlockSpec((tm, d), lambda j, i: (i, 0)),
                  pl.BlockSpec((tm, 3 * BRANCH), lambda j, i: (i, 0)),
                  gate_cols(0), gate_cols(1), gate_cols(2),
                  pl.BlockSpec((3, 1, tn), lambda j, i: (0, 0, j)),
                  pl.BlockSpec((None, None, 3, BRANCH, tn), lambda j, i: (layer, j, 0, 0, 0))],
        out_specs=pl.BlockSpec((tm, tn), lambda j, i: (i, j)),
        compiler_params=_cparams("parallel", "parallel"),
        name="gate_merge",
    )(h, yabc, w_gate, w_gate, w_gate, bg, wb)


def _out_ln_kernel(m_ref, w_ref, x_ref, g_ref, lg_ref, lb_ref, sc_ref, sh_ref, wr_ref,
                   x1_ref, h2_ref, aff_ref, pre_ref, wbuf_ref, wsem, *, nj, tn, alpha, n_experts, layer):
    j = pl.program_id(1)
    step = pl.program_id(0) * nj + j
    total = pl.num_programs(0) * nj

    def w_copy(s):
        return pltpu.make_async_copy(w_ref.at[layer, s % nj], wbuf_ref.at[s % W_RING], wsem.at[s % W_RING])

    @pl.when(step == 0)
    def _():
        for s in range(W_RING - 1):
            w_copy(s).start()

    @pl.when(step + W_RING - 1 < total)
    def _():
        w_copy(step + W_RING - 1).start()

    w_copy(step).wait()
    pre_ref[j] = alpha * x_ref[...] + g_ref[0] * _dot(m_ref[...], wbuf_ref[step % W_RING])

    @pl.when(j == nj - 1)
    def _():
        d = nj * tn
        tot = None
        for jj in range(nj):
            s = jnp.sum(pre_ref[jj], axis=-1, keepdims=True)
            tot = s if tot is None else tot + s
        mu = tot / d
        tot = None
        for jj in range(nj):
            dl = pre_ref[jj] - mu
            s = jnp.sum(dl * dl, axis=-1, keepdims=True)
            tot = s if tot is None else tot + s
        rstd = lax.rsqrt(tot / d + LN_EPS)
        for jj in range(nj):
            cs = slice(jj * tn, (jj + 1) * tn)
            y = (pre_ref[jj] - mu) * rstd * lg_ref[:, cs] + lb_ref[:, cs]
            x1_ref[:, cs] = y
            h2_ref[:, cs] = (y * (1.0 + sc_ref[0, :, cs]) + sh_ref[0, :, cs]).astype(BF16)
        logits = _dot(h2_ref[...], wr_ref[...])
        lane = lax.broadcasted_iota(jnp.int32, logits.shape, 1)
        logits = jnp.where(lane < n_experts, logits, -jnp.inf)
        e = jnp.exp(logits - jnp.max(logits, axis=-1, keepdims=True))
        aff_ref[...] = e / jnp.sum(e, axis=-1, keepdims=True)


def _out_ln(merged, w_out, x, gate, ln_g, ln_b, sc2, sh2, w_router, layer, seq_len, alpha, n_experts):
    t, d = x.shape
    nj, tn = w_out.shape[1], w_out.shape[3]
    tm = min(512, seq_len)
    per = seq_len // tm
    row = pl.BlockSpec((1, d), lambda i, j: (0, 0))
    vec = pl.BlockSpec((1, 1, d), lambda i, j: (i // per, 0, 0))
    return pl.pallas_call(
        functools.partial(_out_ln_kernel, nj=nj, tn=tn, alpha=alpha, n_experts=n_experts, layer=layer),
        out_shape=(jax.ShapeDtypeStruct((t, d), F32),
                   jax.ShapeDtypeStruct((t, d), BF16),
                   jax.ShapeDtypeStruct((t, ROUTER_LANES), F32)),
        grid=(t // tm, nj),
        in_specs=[pl.BlockSpec((tm, d), lambda i, j: (i, 0)),
                  pl.BlockSpec(memory_space=pl.ANY),
                  pl.BlockSpec((tm, tn), lambda i, j: (i, j)),
                  pl.BlockSpec((1, 1, tn), lambda i, j: (i // per, 0, j)),
                  row, row, vec, vec,
                  pl.BlockSpec((d, ROUTER_LANES), lambda i, j: (0, 0))],
        out_specs=(pl.BlockSpec((tm, d), lambda i, j: (i, 0)),
                   pl.BlockSpec((tm, d), lambda i, j: (i, 0)),
                   pl.BlockSpec((tm, ROUTER_LANES), lambda i, j: (i, 0))),
        scratch_shapes=[pltpu.VMEM((nj, tm, tn), F32), pltpu.VMEM((W_RING, d, tn), BF16),
                        pltpu.SemaphoreType.DMA((W_RING,))],
        compiler_params=_cparams("arbitrary", "arbitrary"),
        name="out_ln",
    )(merged, w_out, x, gate, ln_g, ln_b, sc2, sh2, w_router)


def _ffn_kernel(x_ref, w1_ref, w3_ref, w2_ref, tw_ref, tag_ref, o_ref):
    d = x_ref.shape[2]
    x = x_ref[0]
    h1 = _dot(x, w1_ref[0])
    hid = (h1 * _sigmoid(h1)) * _dot(x, w3_ref[0])
    o_ref[0, :, :d] = (_dot(hid.astype(BF16), w2_ref[0]) * tw_ref[0]).astype(BF16)
    o_ref[0, :, d:] = tag_ref[0]


def _expert_ffn(xs, w1, w3, w2, tw, tag, layer):
    e, m, d = xs.shape
    ff = w1.shape[3]
    tm = min(512, m)
    return pl.pallas_call(
        _ffn_kernel,
        out_shape=jax.ShapeDtypeStruct((e, m, d + TAG_LANES), BF16),
        grid=(e, m // tm),
        in_specs=[pl.BlockSpec((1, tm, d), lambda ei, i: (ei, i, 0)),
                  pl.BlockSpec((None, 1, d, ff), lambda ei, i: (layer, ei, 0, 0)),
                  pl.BlockSpec((None, 1, d, ff), lambda ei, i: (layer, ei, 0, 0)),
                  pl.BlockSpec((None, 1, ff, d), lambda ei, i: (layer, ei, 0, 0)),
                  pl.BlockSpec((1, tm, 1), lambda ei, i: (ei, i, 0)),
                  pl.BlockSpec((1, tm, TAG_LANES), lambda ei, i: (ei, i, 0))],
        out_specs=pl.BlockSpec((1, tm, d + TAG_LANES), lambda ei, i: (ei, i, 0)),
        compiler_params=_cparams("parallel", "parallel"),
        name="expert_ffn",
    )(xs, w1, w3, w2, tw, tag)


def _tag_tokens(tags):
    lane = lax.broadcasted_iota(jnp.int32, (SUBLANES, TAG_LANES), 1)
    coef = jnp.where(lane == 0, float(TAG_BASE), jnp.where(lane == 1, 1.0, 0.0)).astype(BF16)
    return lax.dot_general(coef, tags, (((1,), (1,)), ((), ())), preferred_element_type=F32)[0:1]


def _select_rows(tok, valid, first_token, tm):
    ids = (first_token + lax.broadcasted_iota(jnp.int32, (tm, tok.shape[1]), 0)).astype(F32)
    return jnp.where(jnp.where(valid, tok, -1.0) == ids, 1.0, 0.0).astype(BF16)


def _combine_ln2_kernel(s0_ref, cnt_ref, x_ref, oe_ref, g_ref, lg_ref, lb_ref, sc_ref, sh_ref,
                        x2_ref, h_ref, win_ref, xw_ref, acc_ref, sem, xsem,
                        *, alpha, n_exp, win, tm, total_rows):
    i = pl.program_id(0)
    d = x_ref.shape[1]
    slot = i % 2
    last_start = total_rows - win
    cover = win - BF16_ROW_TILE
    first_token = i * tm

    def window_start(first_row):
        start = (jnp.minimum(first_row, last_start) // BF16_ROW_TILE) * BF16_ROW_TILE
        return pl.multiple_of(start, BF16_ROW_TILE)

    def window_copy(start, buf, e):
        return pltpu.make_async_copy(oe_ref.at[pl.ds(start, win)],
                                     win_ref.at[buf, pl.ds(e * win, win)], sem.at[buf, e])

    def fetch(tile, buf):
        for e in range(n_exp):
            window_copy(window_start(s0_ref[tile * n_exp + e]), buf, e).start()

    @pl.when(i == 0)
    def _():
        fetch(0, 0)

    @pl.when(i + 1 < pl.num_programs(0))
    def _():
        fetch(i + 1, 1 - slot)

    for e in range(n_exp):
        window_copy(0, slot, e).wait()

    wl = lax.broadcasted_iota(jnp.int32, (1, n_exp * win), 1)
    row = jnp.zeros_like(wl)
    lo = jnp.zeros_like(wl)
    hi = jnp.zeros_like(wl)
    for e in range(n_exp):
        s0 = s0_ref[i * n_exp + e]
        cnt = cnt_ref[i * n_exp + e]
        in_e = (wl >= e * win) & (wl < (e + 1) * win)
        row = jnp.where(in_e, window_start(s0) + (wl - e * win), row)
        lo = jnp.where(in_e, s0, lo)
        hi = jnp.where(in_e, s0 + jnp.minimum(cnt, cover), hi)
    valid = (row >= lo) & (row < hi)
    tok = _tag_tokens(win_ref[slot, :, d:])
    acc_ref[...] = _dot(_select_rows(tok, valid, first_token, tm), win_ref[slot, :, :d])

    for e in range(n_exp):
        s0 = s0_ref[i * n_exp + e]
        cnt = cnt_ref[i * n_exp + e]

        def extra(k, carry, s0=s0, cnt=cnt):
            lo_k = s0 + k * cover
            start = window_start(lo_k)
            cp = pltpu.make_async_copy(oe_ref.at[pl.ds(start, win)], xw_ref, xsem.at[0])
            cp.start()
            cp.wait()
            row_k = start + lax.broadcasted_iota(jnp.int32, (1, win), 1)
            ok = (row_k >= lo_k) & (row_k < jnp.minimum(lo_k + cover, s0 + cnt))
            sel = _select_rows(_tag_tokens(xw_ref[:, d:]), ok, first_token, tm)
            acc_ref[...] += _dot(sel, xw_ref[:, :d])
            return carry

        lax.fori_loop(1, (cnt + cover - 1) // cover, extra, 0)

    pre = alpha * x_ref[...] + g_ref[0] * acc_ref[...]
    dl = pre - jnp.mean(pre, axis=-1, keepdims=True)
    var = jnp.mean(dl * dl, axis=-1, keepdims=True)
    y = dl * lax.rsqrt(var + LN_EPS) * lg_ref[...] + lb_ref[...]
    x2_ref[...] = y
    h_ref[...] = (y * (1.0 + sc_ref[0]) + sh_ref[0]).astype(BF16)


def _combine_ln2(x, oe, s0, cnt, gate, ln_g, ln_b, sc, sh, seq_len, tm, alpha, n_exp):
    t, d = x.shape
    total_rows = oe.shape[0]
    win = COMBINE_WINDOW
    assert total_rows >= win and total_rows % BF16_ROW_TILE == 0 and t <= TAG_BASE * 256
    per = seq_len // tm
    tile = pl.BlockSpec((tm, d), lambda i, *_: (i, 0))
    row = pl.BlockSpec((1, d), lambda i, *_: (0, 0))
    vec = pl.BlockSpec((1, 1, d), lambda i, *_: (i // per, 0, 0))
    return pl.pallas_call(
        functools.partial(_combine_ln2_kernel, alpha=alpha, n_exp=n_exp, win=win, tm=tm,
                          total_rows=total_rows),
        out_shape=(jax.ShapeDtypeStruct((t, d), F32), jax.ShapeDtypeStruct((t, d), BF16)),
        grid_spec=pltpu.PrefetchScalarGridSpec(
            num_scalar_prefetch=2,
            grid=(t // tm,),
            in_specs=[tile, pl.BlockSpec(memory_space=pl.ANY), vec, row, row, vec, vec],
            out_specs=(tile, tile),
            scratch_shapes=[pltpu.VMEM((2, n_exp * win, d + TAG_LANES), BF16),
                            pltpu.VMEM((win, d + TAG_LANES), BF16),
                            pltpu.VMEM((tm, d), F32),
                            pltpu.SemaphoreType.DMA((2, n_exp)),
                            pltpu.SemaphoreType.DMA((1,))]),
        compiler_params=_cparams("arbitrary"),
        name="combine_ln2",
    )(s0, cnt, x, oe, gate, ln_g, ln_b, sc, sh)


def _mixer_inputs(h, lp, bs, ls, h0):
    proj = _proj(h, lp['w_in'], lp['layer'])
    yf, yb, hfin = _s5(proj.reshape(bs, ls, -1), h0, lp['s5_ops'], col_block=2)
    return proj, yf.reshape(bs * ls, BRANCH), yb.reshape(bs * ls, BRANCH), hfin


def _expert_choice(h2, aff, lp, bs, ls, tm):
    n_exp = lp['w1'].shape[1]
    d = h2.shape[1]
    cap = CAPACITY_FACTOR * ls // n_exp
    m = bs * cap
    a = aff[:, :n_exp].reshape(bs, ls, n_exp)
    top_w, top_idx = lax.top_k(jnp.swapaxes(a, 1, 2), cap)
    top_idx, top_w = lax.sort((top_idx, top_w), dimension=2, num_keys=1)
    gidx = top_idx + (jnp.arange(bs, dtype=top_idx.dtype) * ls)[:, None, None]
    gidx = gidx.transpose(1, 0, 2).reshape(-1)
    xs = h2.at[gidx].get(mode='promise_in_bounds').reshape(n_exp, m, d)
    tw = top_w.transpose(1, 0, 2).reshape(n_exp, m, 1)
    tag = jnp.stack([gidx // TAG_BASE, gidx % TAG_BASE], axis=-1).astype(BF16)
    tag = jnp.pad(tag, ((0, 0), (0, TAG_LANES - 2))).reshape(n_exp, m, TAG_LANES)
    oe = _expert_ffn(xs, lp['w1'], lp['w3'], lp['w2'], tw, tag, lp['layer'])
    tile_start = jnp.arange(ls // tm, dtype=top_idx.dtype) * tm
    below = jnp.sum(top_idx[:, :, None, :] < tile_start[None, None, :, None], axis=-1)
    upto = jnp.concatenate([below[:, :, 1:], jnp.full((bs, n_exp, 1), cap, below.dtype)], axis=2)
    base = (jnp.arange(n_exp) * m)[None, :, None] + (jnp.arange(bs) * cap)[:, None, None]
    s0 = (base + below).transpose(0, 2, 1).reshape(-1).astype(jnp.int32)
    cnt = (upto - below).transpose(0, 2, 1).reshape(-1).astype(jnp.int32)
    return oe.reshape(n_exp * m, d + TAG_LANES), s0, cnt


def _layer(x, h, mod, nxt, lp, bs, ls, seg, h0, alpha):
    proj, yf, yb, hfin = _mixer_inputs(h, lp, bs, ls, h0)
    yabc = _mixers(proj, yf, yb, lp, seg)
    merged = _gate_merge(h, yabc, lp['wg'], lp['bg'], lp['wb'], lp['layer'])
    n_exp = lp['w1'].shape[1]
    x1, h2, aff = _out_ln(merged, lp['w_out'], x, mod[2], lp['ln1_g'], lp['ln1_b'], mod[4], mod[3],
                          lp['w_router'], lp['layer'], ls, alpha, n_exp)
    tm = min(256, ls)
    oe, s0, cnt = _expert_choice(h2, aff, lp, bs, ls, tm)
    x2, hn = _combine_ln2(x1, oe, s0, cnt, mod[5], lp['ln2_g'], lp['ln2_b'], nxt[1], nxt[0],
                          ls, tm, alpha, n_exp)
    return x2, hn, hfin


def kernel(x, c, ctx, c_ctx, ada_w_down, ada_w_up, ada_b, w_in, gmlp_w_s, gmlp_b_s, s5_a_re, s5_a_im, s5_log_step, s5_b_re, s5_b_im, s5_c_re, s5_c_im, s5_d, s5_w_glu, s5_b_glu, pool_w, pool_scale, w_gate, b_gate, w_branch, w_out, ln1_g, ln1_b, w_router, w1, w3, w2, ln2_g, ln2_b):
    bs, n, d = x.shape
    lc = ctx.shape[1]
    depth = w_in.shape[0]
    n_exp = w_router.shape[2]
    alpha = (2.0 * depth) ** 0.25
    assert bs + 1 <= SUBLANES and n % GRID_W == 0

    cond = jnp.zeros((SUBLANES, d), F32).at[:bs].set(c).at[bs].set(c_ctx)
    mods = _ada(cond, ada_w_down, ada_w_up, ada_b).reshape(depth, SUBLANES, N_MOD, d)

    def lat_mod(l):
        return mods[l, :bs].transpose(1, 0, 2)[:, :, None, :]

    def ctx_mod(l):
        return jnp.broadcast_to(mods[l, bs][:, None, None, :], (N_MOD, bs, 1, d))

    tn = COLUMN_BLOCK
    nj = d // tn
    wg_b = _block_columns(w_gate, tn).reshape(depth, 3, nj, d, tn)
    wb_b = _block_columns(w_branch.reshape(depth, 3 * BRANCH, d), tn).reshape(depth, nj, 3, BRANCH, tn)
    wo_b = _block_columns(w_out, tn)
    w1_b, w3_b, w2_b = w1.astype(BF16), w3.astype(BF16), w2.astype(BF16)

    def layer_params(l):
        return dict(
            layer=l, w_in=w_in,
            s5_ops=_s5_operators(s5_a_re[l], s5_a_im[l], s5_log_step[l], s5_b_re[l], s5_b_im[l],
                                 s5_c_re[l], s5_c_im[l]),
            ws=gmlp_w_s[l].astype(BF16),
            bs=jnp.broadcast_to(gmlp_b_s[l][:, :, None], (HEADS, CHUNK, 128)).astype(F32),
            sd=s5_d[l].reshape(1, BRANCH), wglu=s5_w_glu[l].astype(BF16), bglu=s5_b_glu[l].reshape(1, BRANCH),
            pw=pool_w[l].astype(BF16), ps=pool_scale[l].reshape(1, BRANCH),
            wg=wg_b, bg=b_gate[l].reshape(3, 1, d), wb=wb_b,
            w_out=wo_b, ln1_g=ln1_g[l].reshape(1, d), ln1_b=ln1_b[l].reshape(1, d),
            w_router=jnp.zeros((d, ROUTER_LANES), BF16).at[:, :n_exp].set(w_router[l].astype(BF16)),
            w1=w1_b, w3=w3_b, w2=w2_b,
            ln2_g=ln2_g[l].reshape(1, d), ln2_b=ln2_b[l].reshape(1, d))

    xl = x.reshape(bs * n, d)
    xc = ctx.reshape(bs * lc, d)
    m0, c0 = lat_mod(0), ctx_mod(0)
    hl = _modulate(xl, m0[0], m0[1], n)
    hc = _modulate(xc, c0[0], c0[1], lc)
    zero_state = jnp.zeros((S5_GB, 2, SUBLANES, S5_GB * S5_STATE), F32)
    for l in range(depth):
        lp = layer_params(l)
        nl = min(l + 1, depth - 1)
        if l < depth - 1:
            xc, hc, ctx_final = _layer(xc, hc, ctx_mod(l), ctx_mod(nl), lp, bs, lc, lc, zero_state, alpha)
        else:
            ctx_final = _mixer_inputs(hc, lp, bs, lc, zero_state)[-1]
        xl, hl, _ = _layer(xl, hl, lat_mod(l), lat_mod(nl), lp, bs, n, GRID_W, ctx_final, alpha)
    return xl.reshape(bs, n, d)
```
